```python
import functools
import jax
import jax.numpy as jnp
from jax import lax
import numpy as np

D_MODEL = 1024
BATCH = 32
SEQ = 2048
DEPTH = 4

GRID_W = 64
CTX_LEN = 256
D_MIX = D_MODEL
GLA_W = D_MIX // 4
RET_W = D_MIX // 4
MLA_W = D_MIX - GLA_W - RET_W
GLA_DV = 64
GLA_HEADS = GLA_W // GLA_DV
GLA_DK = GLA_DV // 2
GLA_QK = GLA_HEADS * GLA_DK
GLA_GATE_RANK = 16
GLA_TAU = 16.0
RET_DV = 64
RET_HEADS = RET_W // RET_DV
RET_DK = RET_DV // 2
RET_QK = RET_HEADS * RET_DK
MLA_DV = 64
MLA_HEADS = MLA_W // MLA_DV
MLA_D_NOPE = 64
MLA_D_ROPE = 32
MLA_Q_RANK = D_MODEL // 4
MLA_KV_RANK = D_MODEL // 8
MLA_SCALE = (MLA_D_NOPE + MLA_D_ROPE) ** -0.5
CHUNK = 64
Q_BLOCK = 128
D_FF = 128 * ((8 * D_MODEL // 3 + 127) // 128)
CONV_W = 3
ROPE_BASE = 10000.0
EPS = 1e-6
ALPHA = (2 * DEPTH) ** 0.25
BETA = (8 * DEPTH) ** -0.25
ADA_INIT = 0.5
IN_SIZES = (GLA_QK, GLA_QK, GLA_W, 2 * GLA_GATE_RANK, GLA_W,
            RET_QK, RET_QK, RET_W, RET_W,
            MLA_Q_RANK, MLA_KV_RANK, MLA_D_ROPE)
D_IN = sum(IN_SIZES)

kernel_name = "hybrid_gla_retnet_mla_prefix_dit"


def _layer_norm(x, g, b):
    xf = x.astype(jnp.float32)
    mu = jnp.mean(xf, axis=-1, keepdims=True)
    var = jnp.mean(jnp.square(xf - mu), axis=-1, keepdims=True)
    return ((xf - mu) * lax.rsqrt(var + EPS) * g + b).astype(x.dtype)


def _rms_norm(x, g):
    xf = x.astype(jnp.float32)
    return xf * lax.rsqrt(jnp.mean(jnp.square(xf), axis=-1, keepdims=True) + EPS) * g


def _group_norm(x):
    xf = x.astype(jnp.float32)
    mu = jnp.mean(xf, axis=-1, keepdims=True)
    var = jnp.mean(jnp.square(xf - mu), axis=-1, keepdims=True)
    return (xf - mu) * lax.rsqrt(var + EPS)


def _heads(t, n):
    return t.reshape(t.shape[0], t.shape[1], n, -1)


def _merge(t):
    return t.reshape(t.shape[0], t.shape[1], -1)


def _rot_half(x, cos, sin):
    cos = cos.astype(x.dtype)
    sin = sin.astype(x.dtype)
    x1, x2 = jnp.split(x, 2, axis=-1)
    return jnp.concatenate([x1 * cos - x2 * sin, x1 * sin + x2 * cos], axis=-1)


def _axial_rope(x, row_cos, row_sin, col_cos, col_sin):
    xr, xc = jnp.split(x, 2, axis=-1)
    return jnp.concatenate([_rot_half(xr, row_cos, row_sin), _rot_half(xc, col_cos, col_sin)], axis=-1)


def _project(h, w_in):
    p = jnp.einsum('bld,de->ble', h, w_in)
    return jnp.split(p, np.cumsum(IN_SIZES)[:-1].tolist(), axis=-1)


def _to_chunks(t):
    b, l, h, d = t.shape
    return t.reshape(b, l // CHUNK, CHUNK, h, d).transpose(1, 0, 3, 2, 4)


def _from_chunks(t):
    n, b, h, c, d = t.shape
    return t.transpose(1, 0, 3, 2, 4).reshape(b, n * c, h, d)


def _gla_log_gate(lr, w2, b2):
    z = jnp.einsum('blr,re->ble', lr, w2) + b2
    return _heads(jax.nn.log_sigmoid(z.astype(jnp.float32)) / GLA_TAU, GLA_HEADS)


def _gla_scan(q, k, v, log_a, s0):
    f32 = jnp.float32
    qc, kc, vc, ac = tuple(_to_chunks(t.astype(f32)) for t in (q, k, v, log_a))
    tri = jnp.tril(jnp.ones((CHUNK, CHUNK), dtype=bool))[:, :, None]

    def step(s, inp):
        qi, ki, vi, ai = inp
        b = jnp.cumsum(ai, axis=2)
        diff = b[:, :, :, None, :] - b[:, :, None, :, :]
        decay = jnp.exp(jnp.where(tri, diff, -jnp.inf))
        att = jnp.einsum('bhtd,bhsd,bhtsd->bhts', qi, ki, decay)
        o = (jnp.einsum('bhtd,bhde->bhte', qi * jnp.exp(b), s)
             + jnp.einsum('bhts,bhse->bhte', att, vi))
        b_end = b[:, :, -1:, :]
        s = (s * jnp.exp(b_end)[:, :, 0, :, None]
             + jnp.einsum('bhsd,bhse->bhde', ki * jnp.exp(b_end - b), vi))
        return s, o

    s_fin, o = lax.scan(step, s0, (qc, kc, vc, ac))
    return _from_chunks(o), s_fin


def _ret_scan(log_g, q, k, v, s0):
    f32 = jnp.float32
    qc, kc, vc = tuple(_to_chunks(t.astype(f32)) for t in (q, k, v))
    idx = jnp.arange(CHUNK, dtype=f32)
    rel = idx[:, None] - idx[None, :]
    dmat = jnp.where(rel >= 0, jnp.exp(jnp.maximum(rel, 0.0) * log_g[:, None, None]), 0.0)
    q_dec = jnp.exp((idx + 1.0) * log_g[:, None])[..., None]
    k_dec = jnp.exp((CHUNK - 1.0 - idx) * log_g[:, None])[..., None]
    s_dec = jnp.exp(CHUNK * log_g)[:, None, None]

    def step(s, inp):
        qi, ki, vi = inp
        att = jnp.einsum('bhtd,bhsd->bhts', qi, ki) * dmat
        o = (jnp.einsum('bhtd,bhde->bhte', qi * q_dec, s)
             + jnp.einsum('bhts,bhse->bhte', att, vi))
        s = s * s_dec + jnp.einsum('bhsd,bhse->bhde', ki * k_dec, vi)
        return s, o

    s_fin, o = lax.scan(step, s0, (qc, kc, vc))
    return _from_chunks(o), s_fin


def _bidir_prefix(scan_f, scan_b, ctx_f, lat_f, ctx_b, lat_b, s0):
    flip = lambda ts: [jnp.flip(t, axis=1) for t in ts]
    oc_f, sc_f = scan_f(*ctx_f, s0)
    ol_f, _ = scan_f(*lat_f, sc_f)
    oc_b, sc_b = scan_b(*flip(ctx_b), s0)
    ol_b, _ = scan_b(*flip(lat_b), sc_b)
    return oc_f + jnp.flip(oc_b, axis=1), ol_f + jnp.flip(ol_b, axis=1)


def _mla_attend(qn, qr, kn, kr, v):
    s = (jnp.einsum('bqhd,bkhd->bhqk', qn, kn)
         + jnp.einsum('bqhr,bkr->bhqk', qr, kr)).astype(jnp.float32) * MLA_SCALE
    p = jax.nn.softmax(s, axis=-1).astype(v.dtype)
    return jnp.einsum('bhqk,bkhe->bqhe', p, v)


def _mla_blocks(qn, qr, kn, kr, v):
    b, s = qn.shape[0], qn.shape[1]
    nb = s // Q_BLOCK
    blk = lambda t: jnp.moveaxis(t.reshape(b, nb, Q_BLOCK, *t.shape[2:]), 1, 0)
    out = lax.map(lambda qs: _mla_attend(qs[0], qs[1], kn, kr, v), (blk(qn), blk(qr)))
    return jnp.moveaxis(out, 0, 1).reshape(b, s, MLA_HEADS, MLA_DV)


def _token_mixers(h_c, h_l, rope, w_in, gla_gate_w, gla_gate_b, gla_norm_g, ret_decay,
                  mla_q_norm_g, mla_kv_norm_g, mla_w_uq, mla_w_uk, mla_w_uv, need_ctx):
    ret_cos, ret_sin, row_cos, row_sin, col_cos, col_sin = rope
    b = h_l.shape[0]
    pc = _project(h_c, w_in)
    pl = _project(h_l, w_in)

    def gla_inputs(p):
        q = _heads(p[0], GLA_HEADS) * (GLA_DK ** -0.5)
        k = _heads(p[1], GLA_HEADS)
        v = _heads(p[2], GLA_HEADS)
        lr_f, lr_b = jnp.split(p[3], 2, axis=-1)
        a_f = _gla_log_gate(lr_f, gla_gate_w[0], gla_gate_b[0])
        a_b = _gla_log_gate(lr_b, gla_gate_w[1], gla_gate_b[1])
        return (q, k, v, a_f), (q, k, v, a_b)

    gc_f, gc_b = gla_inputs(pc)
    gl_f, gl_b = gla_inputs(pl)
    s0_gla = jnp.zeros((b, GLA_HEADS, GLA_DK, GLA_DV), jnp.float32)
    gla_c, gla_l = _bidir_prefix(_gla_scan, _gla_scan, gc_f, gl_f, gc_b, gl_b, s0_gla)
    gla_out = lambda o, p: _merge(_rms_norm(o, gla_norm_g) * jax.nn.silu(_heads(p[4], GLA_HEADS)))

    log_g = jax.nn.log_sigmoid(ret_decay.astype(jnp.float32))

    def ret_inputs(p, rotate):
        q = _heads(p[5], RET_HEADS)
        k = _heads(p[6], RET_HEADS) * (RET_DK ** -0.5)
        v = _heads(p[7], RET_HEADS)
        if rotate:
            q = _rot_half(q, ret_cos[:, None], ret_sin[:, None])
            k = _rot_half(k, ret_cos[:, None], ret_sin[:, None])
        return (q, k, v)

    rc = ret_inputs(pc, False)
    rl = ret_inputs(pl, True)
    s0_ret = jnp.zeros((b, RET_HEADS, RET_DK, RET_DV), jnp.float32)
    ret_c, ret_l = _bidir_prefix(functools.partial(_ret_scan, log_g[0]),
                                 functools.partial(_ret_scan, log_g[1]),
                                 rc, rl, rc, rl, s0_ret)
    ret_out = lambda o, p: _merge(_group_norm(o) * jax.nn.silu(_heads(p[8], RET_HEADS)))

    def mla_inputs(p, rotate):
        cq = _rms_norm(p[9], mla_q_norm_g)
        q = _heads(jnp.einsum('blr,re->ble', cq, mla_w_uq), MLA_HEADS)
        qn, qr = q[..., :MLA_D_NOPE], q[..., MLA_D_NOPE:]
        ckv = _rms_norm(p[10], mla_kv_norm_g)
        kn = _heads(jnp.einsum('blr,re->ble', ckv, mla_w_uk), MLA_HEADS)
        v = _heads(jnp.einsum('blr,re->ble', ckv, mla_w_uv), MLA_HEADS)
        kr = p[11]
        if rotate:
            qr = _axial_rope(qr, row_cos[:, None], row_sin[:, None], col_cos[:, None], col_sin[:, None])
            kr = _axial_rope(kr, row_cos, row_sin, col_cos, col_sin)
        return qn, qr, kn, kr, v

    qn_c, qr_c, kn_c, kr_c, v_c = mla_inputs(pc, False)
    qn_l, qr_l, kn_l, kr_l, v_l = mla_inputs(pl, True)
    mla_l = _mla_blocks(qn_l, qr_l,
                        jnp.concatenate([kn_l, kn_c], axis=1),
                        jnp.concatenate([kr_l, kr_c], axis=1),
                        jnp.concatenate([v_l, v_c], axis=1))
    m_l = jnp.concatenate([gla_out(gla_l, pl), ret_out(ret_l, pl), _merge(mla_l)], axis=-1).astype(h_l.dtype)
    if not need_ctx:
        return None, m_l
    mla_c = _mla_attend(qn_c, qr_c, kn_c, kr_c, v_c)
    m_c = jnp.concatenate([gla_out(gla_c, pc), ret_out(ret_c, pc), _merge(mla_c)], axis=-1).astype(h_c.dtype)
    return m_c, m_l


def _conv_ffn(h, w_up, conv_w, conv_b, w_down):
    u = jnp.einsum('bld,df->blf', h, w_up)
    l = u.shape[1]
    pad = CONV_W // 2
    up = jnp.pad(u, ((0, 0), (pad, pad), (0, 0)))
    u = sum(up[:, j:j + l] * conv_w[j] for j in range(CONV_W)) + conv_b
    a, g = jnp.split(u, 2, axis=-1)
    return jnp.einsum('blf,fd->bld', jax.nn.silu(a) * g, w_down)


def setup_inputs(seed: int = 0) -> dict:
    key = jax.random.key(seed)
    ks = jax.random.split(key, 28)
    f32 = jnp.float32
    nrm = lambda k, shape, s: jax.random.normal(k, shape, f32) * s
    L = DEPTH
    ret_base = jnp.log(2.0 ** (5.0 + jnp.arange(RET_HEADS, dtype=f32)) - 1.0)
    return {
        "x": nrm(ks[0], (BATCH, SEQ, D_MODEL), 1.0),
        "c": nrm(ks[1], (BATCH, D_MODEL), 1.0),
        "ctx": nrm(ks[2], (BATCH, CTX_LEN, D_MODEL), 1.0),
        "c_ctx": nrm(ks[3], (D_MODEL,), 1.0),
        "ada_w": nrm(ks[4], (L, D_MODEL, 6 * D_MODEL), ADA_INIT * D_MODEL ** -0.5),
        "ada_b": nrm(ks[5], (L, 6 * D_MODEL), 0.01),
        "w_in": nrm(ks[6], (L, D_MODEL, D_IN), D_MODEL ** -0.5),
        "gla_gate_w": nrm(ks[7], (L, 2, GLA_GATE_RANK, GLA_QK), GLA_GATE_RANK ** -0.5),
        "gla_gate_b": nrm(ks[8], (L, 2, GLA_QK), 0.1),
        "gla_norm_g": 1.0 + nrm(ks[9], (L, GLA_DV), 0.02),
        "ret_decay": ret_base + nrm(ks[10], (L, 2, RET_HEADS), 0.1),
        "mla_q_norm_g": 1.0 + nrm(ks[11], (L, MLA_Q_RANK), 0.02),
        "mla_kv_norm_g": 1.0 + nrm(ks[12], (L, MLA_KV_RANK), 0.02),
        "mla_w_uq": nrm(ks[13], (L, MLA_Q_RANK, MLA_HEADS * (MLA_D_NOPE + MLA_D_ROPE)), MLA_Q_RANK ** -0.5),
        "mla_w_uk": nrm(ks[14], (L, MLA_KV_RANK, MLA_HEADS * MLA_D_NOPE), MLA_KV_RANK ** -0.5),
        "mla_w_uv": nrm(ks[15], (L, MLA_KV_RANK, MLA_HEADS * MLA_DV), MLA_KV_RANK ** -0.5),
        "w_out": nrm(ks[16], (L, D_MIX, D_MODEL), BETA * D_MIX ** -0.5),
        "ln1_g": 1.0 + nrm(ks[17], (L, D_MODEL), 0.02),
        "ln1_b": nrm(ks[18], (L, D_MODEL), 0.02),
        "ffn_up": nrm(ks[19], (L, D_MODEL, 2 * D_FF), D_MODEL ** -0.5),
        "ffn_conv_w": nrm(ks[20], (L, CONV_W, 2 * D_FF), CONV_W ** -0.5),
        "ffn_conv_b": nrm(ks[21], (L, 2 * D_FF), 0.01),
        "ffn_down": nrm(ks[22], (L, D_FF, D_MODEL), BETA * D_FF ** -0.5),
        "ln2_g": 1.0 + nrm(ks[23], (L, D_MODEL), 0.02),
        "ln2_b": nrm(ks[24], (L, D_MODEL), 0.02),
    }


def reference(x, c, ctx, c_ctx, ada_w, ada_b, w_in, gla_gate_w, gla_gate_b, gla_norm_g, ret_decay,
              mla_q_norm_g, mla_kv_norm_g, mla_w_uq, mla_w_uk, mla_w_uv, w_out, ln1_g, ln1_b,
              ffn_up, ffn_conv_w, ffn_conv_b, ffn_down, ln2_g, ln2_b):
    f32 = jnp.float32
    seq = x.shape[1]
    rows_n = seq // GRID_W
    rows = jnp.repeat(jnp.arange(rows_n, dtype=f32), GRID_W)
    cols = jnp.tile(jnp.arange(GRID_W, dtype=f32), rows_n)
    pos = jnp.arange(seq, dtype=f32)
    ret_inv = 1.0 / (ROPE_BASE ** jnp.linspace(0.0, 1.0, RET_DK // 2, dtype=f32))
    ret_ang = pos[:, None] * ret_inv
    n_ax = MLA_D_ROPE // 4
    ax_inv = ROPE_BASE ** (-jnp.arange(n_ax, dtype=f32) / n_ax)
    row_ang = rows[:, None] * ax_inv
    col_ang = cols[:, None] * ax_inv
    rope = (jnp.cos(ret_ang), jnp.sin(ret_ang), jnp.cos(row_ang), jnp.sin(row_ang),
            jnp.cos(col_ang), jnp.sin(col_ang))

    for i in range(DEPTH):
        need_ctx = i < DEPTH - 1
        mod_l = jnp.einsum('bd,de->be', jax.nn.silu(c), ada_w[i]) + ada_b[i]
        mod_c = jnp.einsum('d,de->e', jax.nn.silu(c_ctx), ada_w[i]) + ada_b[i]
        sh1_l, sc1_l, g1_l, sh2_l, sc2_l, g2_l = [m[:, None, :] for m in jnp.split(mod_l, 6, axis=-1)]
        sh1_c, sc1_c, g1_c, sh2_c, sc2_c, g2_c = jnp.split(mod_c, 6, axis=-1)

        h_l = x * (1.0 + sc1_l) + sh1_l
        h_c = ctx * (1.0 + sc1_c) + sh1_c
        m_c, m_l = _token_mixers(h_c, h_l, rope, w_in[i], gla_gate_w[i], gla_gate_b[i], gla_norm_g[i],
                                 ret_decay[i], mla_q_norm_g[i], mla_kv_norm_g[i], mla_w_uq[i],
                                 mla_w_uk[i], mla_w_uv[i], need_ctx)
        x = _layer_norm(ALPHA * x + g1_l * jnp.einsum('ble,ed->bld', m_l, w_out[i]), ln1_g[i], ln1_b[i])
        f_l = _conv_ffn(x * (1.0 + sc2_l) + sh2_l, ffn_up[i], ffn_conv_w[i], ffn_conv_b[i], ffn_down[i])
        x = _layer_norm(ALPHA * x + g2_l * f_l, ln2_g[i], ln2_b[i])
        if need_ctx:
            ctx = _layer_norm(ALPHA * ctx + g1_c * jnp.einsum('ble,ed->bld', m_c, w_out[i]), ln1_g[i], ln1_b[i])
            f_c = _conv_ffn(ctx * (1.0 + sc2_c) + sh2_c, ffn_up[i], ffn_conv_w[i], ffn_conv_b[i], ffn_down[i])
            ctx = _layer_norm(ALPHA * ctx + g2_c * f_c, ln2_g[i], ln2_b[i])
    return x
```

```python
import functools

import numpy as np
import jax
import jax.numpy as jnp
from jax import lax
from jax.experimental import pallas as pl
from jax.experimental.pallas import tpu as pltpu

F32 = jnp.float32
BF16 = jnp.bfloat16

D_MODEL = 1024
DEPTH = 4
GRID_W = 64
N_HEADS = 4
DK = 32
DV = 64
QK_W = N_HEADS * DK
V_W = N_HEADS * DV
GATE_RANK = 16
GLA_TAU = 16.0
MLA_HEADS = 8
MLA_D_NOPE = 64
MLA_D_ROPE = 32
MLA_DV = 64
MLA_Q_RANK = 256
MLA_KV_RANK = 128
MLA_HEAD_PAD = 128
MLA_SCALE = (MLA_D_NOPE + MLA_D_ROPE) ** -0.5
CHUNK = 64
D_FF = 2816
ROPE_BASE = 10000.0
EPS = 1e-6
ALPHA = (2 * DEPTH) ** 0.25
IN_SIZES = (QK_W, QK_W, V_W, 2 * GATE_RANK, V_W, QK_W, QK_W, V_W, V_W,
            MLA_Q_RANK, MLA_KV_RANK, MLA_D_ROPE)
IN_OFFS = tuple(int(o) for o in np.concatenate([[0], np.cumsum(IN_SIZES)]))

P_GQ, P_GK, P_GV, P_GG = 0, 128, 256, 512
P_RQ, P_RK, P_RV, P_RG = 768, 896, 1024, 1280
P_CQ, P_CKV, P_MISC = 1536, 1792, 1920
P_RQS, P_RKS = 2048, 2176
P_W = 2304

TM = 256
HALO = 8
FF_CHUNK = 256
VMEM_LIMIT = 56 * 1024 * 1024


def _cparams(sem):
    return pltpu.CompilerParams(dimension_semantics=sem, vmem_limit_bytes=VMEM_LIMIT)


def _split3(x):
    hi = x.astype(BF16)
    r1 = x - hi.astype(F32)
    mid = r1.astype(BF16)
    lo = (r1 - mid.astype(F32)).astype(BF16)
    return hi, mid, lo


def _dot_exact_lhs01(m01, x):
    hi, mid, lo = _split3(x)
    d = lambda t: jnp.dot(m01, t, preferred_element_type=F32)
    return d(hi) + d(mid) + d(lo)


def _dot_exact_rhs01(x, m01):
    hi, mid, lo = _split3(x)
    d = lambda t: jnp.dot(t, m01, preferred_element_type=F32)
    return d(hi) + d(mid) + d(lo)


def _log_sigmoid(z):
    return jnp.minimum(z, 0.0) - jnp.log1p(jnp.exp(-jnp.abs(z)))


def _silu(x):
    return x * jax.nn.sigmoid(x)


def _layer_norm(x, g, b):
    mu = jnp.mean(x, axis=-1, keepdims=True)
    xc = x - mu
    var = jnp.mean(xc * xc, axis=-1, keepdims=True)
    return xc * lax.rsqrt(var + EPS) * g + b


def _rms_norm(x, g):
    return x * lax.rsqrt(jnp.mean(x * x, axis=-1, keepdims=True) + EPS) * g


def _dot_nt(a, b):
    return lax.dot_general(a, b, (((1,), (1,)), ((), ())), preferred_element_type=F32)


def _dot_tn(a, b):
    return lax.dot_general(a, b, (((0,), (0,)), ((), ())), preferred_element_type=F32)


def _ada_kernel(c_ref, w_ref, b_ref, o_ref):
    a = _silu(c_ref[...]).astype(BF16)
    o_ref[0] = jnp.dot(a, w_ref[0].astype(BF16), preferred_element_type=F32) + b_ref[0]


def _ada_call(c_all, ada_w, ada_b):
    depth, d, e = ada_w.shape
    rows = c_all.shape[0]
    bn = 1024
    return pl.pallas_call(
        _ada_kernel,
        grid=(depth, e // bn),
        in_specs=[pl.BlockSpec((rows, d), lambda l, n: (0, 0)),
                  pl.BlockSpec((1, d, bn), lambda l, n: (l, 0, n)),
                  pl.BlockSpec((1, 1, bn), lambda l, n: (l, 0, n))],
        out_specs=pl.BlockSpec((1, rows, bn), lambda l, n: (l, 0, n)),
        out_shape=jax.ShapeDtypeStruct((depth, rows, e), F32),
        compiler_params=_cparams(("arbitrary", "arbitrary")),
        name="ada_mod",
    )(c_all, ada_w, ada_b.reshape(depth, 1, e))


def _proj_kernel(x_ref, mod_ref, tab_ref, w_in_ref, w_gate_ref, b_gate_ref, qn_g_ref, kvn_g_ref,
                 w_uq_ref, w_kv_ref,
                 g1_ref, gv_ref, gg_ref, r1_ref, rv_ref, rg_ref, mq_ref, mkv_ref):
    d = D_MODEL
    x = x_ref[0]
    mod = mod_ref[0]
    h = (x * (1.0 + mod[:, d:2 * d]) + mod[:, 0:d]).astype(BF16)
    p = jnp.dot(h, w_in_ref[...], preferred_element_type=F32)

    ret_cos = tab_ref[:, 0:128]
    ret_sin = tab_ref[:, 128:256]
    q_cos = tab_ref[:, 256:384]
    q_sin = tab_ref[:, 384:512]
    misc_tab = tab_ref[:, 512:640]

    misc = p[:, P_MISC:P_MISC + 128]
    z = jnp.dot(misc.astype(BF16), w_gate_ref[...], preferred_element_type=F32) + b_gate_ref[...]
    log_a = _log_sigmoid(z) / GLA_TAU
    g1_ref[0, :, 0:128] = p[:, P_GQ:P_GQ + 128] * (DK ** -0.5)
    g1_ref[0, :, 128:256] = p[:, P_GK:P_GK + 128]
    g1_ref[0, :, 256:512] = log_a
    gv_ref[0] = p[:, P_GV:P_GV + V_W].astype(BF16)
    gg_ref[0] = p[:, P_GG:P_GG + V_W]

    r1_ref[0, :, 0:128] = p[:, P_RQ:P_RQ + 128] * ret_cos + p[:, P_RQS:P_RQS + 128] * ret_sin
    ks = DK ** -0.5
    r1_ref[0, :, 128:256] = ((p[:, P_RK:P_RK + 128] * ks) * ret_cos
                             + (p[:, P_RKS:P_RKS + 128] * ks) * ret_sin)
    rv_ref[0] = p[:, P_RV:P_RV + V_W].astype(BF16)
    rg_ref[0] = p[:, P_RG:P_RG + V_W]

    cq = _rms_norm(p[:, P_CQ:P_CQ + MLA_Q_RANK], qn_g_ref[...]).astype(BF16)
    qq = jnp.dot(cq, w_uq_ref[...], preferred_element_type=F32)
    qw = MLA_HEADS * MLA_HEAD_PAD
    for hh in range(MLA_HEADS):
        lo = hh * MLA_HEAD_PAD
        mq_ref[0, :, lo:lo + MLA_HEAD_PAD] = (
            qq[:, lo:lo + MLA_HEAD_PAD] * q_cos + qq[:, qw + lo:qw + lo + MLA_HEAD_PAD] * q_sin
        ).astype(BF16)

    ckv = _rms_norm(p[:, P_CKV:P_CKV + MLA_KV_RANK], kvn_g_ref[...])
    prod = misc * misc_tab
    kr = prod + pltpu.roll(prod, 128 - MLA_D_ROPE, axis=1)
    lhs = jnp.concatenate([ckv, kr], axis=1).astype(BF16)
    mkv_ref[0] = jnp.dot(lhs, w_kv_ref[...], preferred_element_type=F32).astype(BF16)


def _proj_call(xa, mod_l, tab, wl, n_ctx_tiles):
    bsz, t_all, d = xa.shape
    nt = t_all // TM
    tile = lambda w: pl.BlockSpec((1, TM, w), lambda b, t: (b, t, 0))
    full = lambda a: pl.BlockSpec(a.shape, lambda b, t: (0,) * a.ndim)
    sds = lambda w, dt: jax.ShapeDtypeStruct((bsz, t_all, w), dt)
    weights = (wl["w_in"], wl["w_gate"], wl["b_gate"], wl["qn_g"], wl["kvn_g"], wl["w_uq"], wl["w_kv"])
    return pl.pallas_call(
        _proj_kernel,
        grid=(bsz, nt),
        in_specs=[tile(d),
                  pl.BlockSpec((1, 1, 6 * d), lambda b, t: (2 * b + (t >= n_ctx_tiles).astype(jnp.int32), 0, 0)),
                  pl.BlockSpec((TM, tab.shape[1]), lambda b, t: (t, 0))]
                 + [full(w) for w in weights],
        out_specs=[tile(512), tile(V_W), tile(V_W), tile(256), tile(V_W), tile(V_W),
                   tile(MLA_HEADS * MLA_HEAD_PAD), tile(2048)],
        out_shape=[sds(512, F32), sds(V_W, BF16), sds(V_W, F32), sds(256, F32), sds(V_W, BF16),
                   sds(V_W, F32), sds(MLA_HEADS * MLA_HEAD_PAD, BF16), sds(2048, BF16)],
        compiler_params=_cparams(("arbitrary", "arbitrary")),
        name="proj",
    )(xa, mod_l, tab, *weights)


def _scan_masks():
    c = CHUNK
    r = lax.broadcasted_iota(jnp.int32, (N_HEADS * c, c), 0)
    s = lax.broadcasted_iota(jnp.int32, (N_HEADS * c, c), 1)
    t = r % c
    causal_f = s <= t
    causal_b = s >= t
    r2 = lax.broadcasted_iota(jnp.int32, (N_HEADS * c, QK_W), 0)
    l2 = lax.broadcasted_iota(jnp.int32, (N_HEADS * c, QK_W), 1)
    head_qk = (r2 // c) == (l2 // DK)
    r3 = lax.broadcasted_iota(jnp.int32, (N_HEADS * c, V_W), 0)
    l3 = lax.broadcasted_iota(jnp.int32, (N_HEADS * c, V_W), 1)
    head_v = (r3 // c) == (l3 // DV)
    return causal_f, causal_b, head_qk, head_v


def _stack_heads(x, head_qk):
    return jnp.where(head_qk, jnp.concatenate([x] * N_HEADS, axis=0), 0.0)


def _intra(att, v, head_v):
    c = CHUNK
    pf = jnp.where(head_v, jnp.dot(att.astype(BF16), v, preferred_element_type=F32), 0.0)
    return pf[0:c] + pf[c:2 * c] + pf[2 * c:3 * c] + pf[3 * c:4 * c]


def _run_scan(chunk_fn, s_ref, n_ctx_chunks, n_chunks):
    def rows(ci):
        return pl.multiple_of(ci * CHUNK, CHUNK)

    s_ref[...] = jnp.zeros_like(s_ref)

    def fwd_body(i, carry):
        chunk_fn(rows(i), True)
        return carry
    lax.fori_loop(0, n_chunks, fwd_body, 0)

    s_ref[...] = jnp.zeros_like(s_ref)

    def bwd_ctx(i, carry):
        chunk_fn(rows(n_ctx_chunks - 1 - i), False)
        return carry
    lax.fori_loop(0, n_ctx_chunks, bwd_ctx, 0)

    def bwd_lat(i, carry):
        chunk_fn(rows(n_chunks - 1 - i), False)
        return carry
    lax.fori_loop(0, n_chunks - n_ctx_chunks, bwd_lat, 0)


def _group_ones():
    r = lax.broadcasted_iota(jnp.int32, (V_W, V_W), 0)
    l = lax.broadcasted_iota(jnp.int32, (V_W, V_W), 1)
    return jnp.where((r // DV) == (l // DV), 1.0, 0.0).astype(BF16)


def _gla_kernel(g1_ref, gv_ref, gg_ref, ng_ref, out_ref, oacc_ref, s_ref, *, n_ctx_chunks, n_chunks):
    c = CHUNK
    causal_f, causal_b, head_qk, head_v = _scan_masks()
    ri = lax.broadcasted_iota(jnp.int32, (c, c), 0)
    si = lax.broadcasted_iota(jnp.int32, (c, c), 1)
    tri_f = jnp.where(si <= ri, 1.0, 0.0).astype(BF16)
    tri_b = jnp.where(si >= ri, 1.0, 0.0).astype(BF16)

    def chunk_fn(r0, fwd):
        q = g1_ref[0, pl.ds(r0, c), 0:128]
        k = g1_ref[0, pl.ds(r0, c), 128:256]
        a = g1_ref[0, pl.ds(r0, c), 256:384] if fwd else g1_ref[0, pl.ds(r0, c), 384:512]
        v = gv_ref[0, pl.ds(r0, c), :]
        b = _dot_exact_lhs01(tri_f if fwd else tri_b, a)
        b_mid = b[c // 2:c // 2 + 1, :]
        b_end = b[c - 1:c, :] if fwd else b[0:1, :]
        qe = q * jnp.exp(b - b_mid)
        ke = k * jnp.exp(b_mid - b)
        att = _dot_nt(_stack_heads(qe, head_qk).astype(BF16), ke.astype(BF16))
        att = jnp.where(causal_f if fwd else causal_b, att, 0.0)
        s = s_ref[...]
        o = _intra(att, v, head_v) + _dot_nt((q * jnp.exp(b)).astype(BF16), s.astype(BF16))
        if fwd:
            oacc_ref[pl.ds(r0, c), :] = o
        else:
            oacc_ref[pl.ds(r0, c), :] += o
        upd = _dot_tn(v, (k * jnp.exp(b_end - b)).astype(BF16))
        s_ref[...] = s * jnp.exp(b_end) + jnp.where(head_qk, upd, 0.0)

    _run_scan(chunk_fn, s_ref, n_ctx_chunks, n_chunks)

    ones_g = _group_ones()
    ng = ng_ref[...]

    def norm_body(i, carry):
        r0 = pl.multiple_of(i * TM, TM)
        o = oacc_ref[pl.ds(r0, TM), :]
        ms = _dot_exact_rhs01(o * o, ones_g) * (1.0 / DV)
        y = o * lax.rsqrt(ms + EPS) * ng * _silu(gg_ref[0, pl.ds(r0, TM), :])
        out_ref[0, pl.ds(r0, TM), :] = y.astype(BF16)
        return carry
    lax.fori_loop(0, (n_chunks * c) // TM, norm_body, 0)


def _ret_kernel(r1_ref, rv_ref, rg_ref, dec_rows_ref, dec_lanes_ref, out_ref, oacc_ref, s_ref, *,
                n_ctx_chunks, n_chunks):
    c = CHUNK
    causal_f, causal_b, head_qk, head_v = _scan_masks()
    r = lax.broadcasted_iota(jnp.int32, (N_HEADS * c, c), 0)
    sidx = lax.broadcasted_iota(jnp.int32, (N_HEADS * c, c), 1)
    tidx = r % c
    tl = lax.broadcasted_iota(jnp.int32, (c, QK_W), 0).astype(F32)

    consts = []
    for di in range(2):
        fwd = di == 0
        lg_rows = _log_sigmoid(dec_rows_ref[di])
        lg_lanes = _log_sigmoid(dec_lanes_ref[di])
        rel = (tidx - sidx) if fwd else (sidx - tidx)
        dmat = jnp.where(rel >= 0, jnp.exp(jnp.maximum(rel, 0).astype(F32) * lg_rows), 0.0)
        n = tl if fwd else (c - 1.0) - tl
        q_dec = jnp.exp((n + 1.0) * lg_lanes)
        k_dec = jnp.exp((c - 1.0 - n) * lg_lanes)
        s_dec = jnp.exp(float(c) * lg_lanes)
        consts.append((dmat, q_dec, k_dec, s_dec))

    def chunk_fn(r0, fwd):
        dmat, q_dec, k_dec, s_dec = consts[0 if fwd else 1]
        q = r1_ref[0, pl.ds(r0, c), 0:128]
        k = r1_ref[0, pl.ds(r0, c), 128:256]
        v = rv_ref[0, pl.ds(r0, c), :]
        att = _dot_nt(_stack_heads(q, head_qk).astype(BF16), k.astype(BF16)) * dmat
        s = s_ref[...]
        o = _intra(att, v, head_v) + _dot_nt((q * q_dec).astype(BF16), s.astype(BF16))
        if fwd:
            oacc_ref[pl.ds(r0, c), :] = o
        else:
            oacc_ref[pl.ds(r0, c), :] += o
        upd = _dot_tn(v, (k * k_dec).astype(BF16))
        s_ref[...] = s * s_dec + jnp.where(head_qk, upd, 0.0)

    _run_scan(chunk_fn, s_ref, n_ctx_chunks, n_chunks)

    ones_g = _group_ones()

    def norm_body(i, carry):
        r0 = pl.multiple_of(i * TM, TM)
        o = oacc_ref[pl.ds(r0, TM), :]
        mu = _dot_exact_rhs01(o, ones_g) * (1.0 / DV)
        oc = o - mu
        var = _dot_exact_rhs01(oc * oc, ones_g) * (1.0 / DV)
        y = oc * lax.rsqrt(var + EPS) * _silu(rg_ref[0, pl.ds(r0, TM), :])
        out_ref[0, pl.ds(r0, TM), :] = y.astype(BF16)
        return carry
    lax.fori_loop(0, (n_chunks * c) // TM, norm_body, 0)


def _scan_call(kernel, name, seq_inputs, small_inputs, n_ctx_chunks):
    bsz, t_all, _ = seq_inputs[0].shape
    per_b = lambda a: pl.BlockSpec((1,) + a.shape[1:], lambda b: (b, 0, 0))
    full = lambda a: pl.BlockSpec(a.shape, lambda b: (0,) * a.ndim)
    return pl.pallas_call(
        functools.partial(kernel, n_ctx_chunks=n_ctx_chunks, n_chunks=t_all // CHUNK),
        grid=(bsz,),
        in_specs=[per_b(a) for a in seq_inputs] + [full(a) for a in small_inputs],
        out_specs=pl.BlockSpec((1, t_all, V_W), lambda b: (b, 0, 0)),
        out_shape=jax.ShapeDtypeStruct((bsz, t_all, V_W), BF16),
        scratch_shapes=[pltpu.VMEM((t_all, V_W), F32), pltpu.VMEM((N_HEADS * DV, QK_W), F32)],
        compiler_params=_cparams(("arbitrary",)),
        name=name,
    )(*seq_inputs, *small_inputs)


def _mla_kernel(q_ref, kv_ref, o_ref, *, n_ctx, t_off):
    t = pl.program_id(1) + t_off
    kw = MLA_HEADS * MLA_HEAD_PAD
    vw = MLA_HEADS * MLA_DV

    def attend(n_keys):
        for pair in range(MLA_HEADS // 2):
            acc = None
            for hh in range(2):
                lo = (2 * pair + hh) * MLA_HEAD_PAD
                s = _dot_nt(q_ref[0, :, lo:lo + MLA_HEAD_PAD], kv_ref[0, 0:n_keys, lo:lo + MLA_HEAD_PAD])
                s = s * MLA_SCALE
                e = jnp.exp(s - jnp.max(s, axis=-1, keepdims=True))
                p = e * (1.0 / jnp.sum(e, axis=-1, keepdims=True))
                vlo = kw + hh * vw + pair * 128
                pv = jnp.dot(p.astype(BF16), kv_ref[0, 0:n_keys, vlo:vlo + 128], preferred_element_type=F32)
                acc = pv if acc is None else acc + pv
            o_ref[0, :, pair * 128:(pair + 1) * 128] = acc.astype(BF16)

    n_ctx_tiles = n_ctx // TM

    @pl.when(t < n_ctx_tiles)
    def _():
        attend(n_ctx)

    @pl.when(t >= n_ctx_tiles)
    def _():
        attend(kv_ref.shape[1])


def _mla_call(mq, mkv, n_ctx, t_off):
    bsz, t_all, qw = mq.shape
    nt = t_all // TM - t_off
    return pl.pallas_call(
        functools.partial(_mla_kernel, n_ctx=n_ctx, t_off=t_off),
        grid=(bsz, nt),
        in_specs=[pl.BlockSpec((1, TM, qw), lambda b, t: (b, t + t_off, 0)),
                  pl.BlockSpec((1, t_all, mkv.shape[2]), lambda b, t: (b, 0, 0))],
        out_specs=pl.BlockSpec((1, TM, MLA_HEADS * MLA_DV), lambda b, t: (b, t, 0)),
        out_shape=jax.ShapeDtypeStruct((bsz, nt * TM, MLA_HEADS * MLA_DV), BF16),
        compiler_params=_cparams(("arbitrary", "arbitrary")),
        name="mla_attn",
    )(mq, mkv)


def _outproj_kernel(x_ref, mg_ref, mr_ref, mm_ref, mod_ref, w_ref, g_ref, b_ref, o_ref):
    d = D_MODEL
    w = V_W
    y = (jnp.dot(mg_ref[0], w_ref[0:w, :], preferred_element_type=F32)
         + jnp.dot(mr_ref[0], w_ref[w:2 * w, :], preferred_element_type=F32)
         + jnp.dot(mm_ref[0], w_ref[2 * w:, :], preferred_element_type=F32))
    g1 = mod_ref[0][:, 2 * d:3 * d]
    o_ref[0] = _layer_norm(ALPHA * x_ref[0] + g1 * y, g_ref[...], b_ref[...])


def _outproj_call(xa, m_gla, m_ret, m_mla, mod_l, wl, n_ctx_tiles, t_off):
    bsz, t_all, d = xa.shape
    nt = t_all // TM - t_off
    src = lambda w: pl.BlockSpec((1, TM, w), lambda b, t: (b, t + t_off, 0))
    full = lambda a: pl.BlockSpec(a.shape, lambda b, t: (0,) * a.ndim)
    return pl.pallas_call(
        _outproj_kernel,
        grid=(bsz, nt),
        in_specs=[src(d), src(V_W), src(V_W),
                  pl.BlockSpec((1, TM, m_mla.shape[2]), lambda b, t: (b, t, 0)),
                  pl.BlockSpec((1, 1, 6 * d),
                               lambda b, t: (2 * b + (t + t_off >= n_ctx_tiles).astype(jnp.int32), 0, 0)),
                  full(wl["w_out"]), full(wl["ln1_g"]), full(wl["ln1_b"])],
        out_specs=pl.BlockSpec((1, TM, d), lambda b, t: (b, t, 0)),
        out_shape=jax.ShapeDtypeStruct((bsz, nt * TM, d), F32),
        compiler_params=_cparams(("arbitrary", "arbitrary")),
        name="out_proj_ln1",
    )(xa, m_gla, m_ret, m_mla, mod_l, wl["w_out"], wl["ln1_g"], wl["ln1_b"])


def _ffn_kernel(xp_ref, x_ref, xn_ref, mod_ref, up_ref, cw_ref, cb_ref, down_ref, g_ref, b_ref, o_ref, *,
                seq_starts, seq_ends):
    d = D_MODEL
    t = pl.program_id(1)
    mod = mod_ref[0]
    sh2, sc2, g2 = mod[:, 3 * d:4 * d], mod[:, 4 * d:5 * d], mod[:, 5 * d:6 * d]
    x1 = x_ref[0]
    xs = jnp.concatenate([xp_ref[0], x1, xn_ref[0]], axis=0)
    h2 = (xs * (1.0 + sc2) + sh2).astype(BF16)

    is_start = functools.reduce(jnp.logical_or, [t == s for s in seq_starts]).astype(jnp.int32)
    is_end = functools.reduce(jnp.logical_or, [t == e for e in seq_ends]).astype(jnp.int32)
    row = lax.broadcasted_iota(jnp.int32, (TM + 2 * HALO, 1), 0)
    keep = jnp.logical_and(row >= HALO * is_start, row < HALO + TM + HALO * (1 - is_end))

    n_rows = TM + 2 * HALO

    def conv(u, lo):
        u = jnp.where(keep, u, 0.0)
        prev = pltpu.roll(u, 1, axis=0)[HALO:HALO + TM]
        nxt = pltpu.roll(u, n_rows - 1, axis=0)[HALO:HALO + TM]
        cw = cw_ref[:, lo:lo + FF_CHUNK]
        return (prev * cw[0:1] + u[HALO:HALO + TM] * cw[1:2] + nxt * cw[2:3]
                + cb_ref[:, lo:lo + FF_CHUNK])

    acc = jnp.zeros((TM, d), F32)
    for j in range(D_FF // FF_CHUNK):
        lo_a = j * FF_CHUNK
        lo_g = D_FF + j * FF_CHUNK
        ua = conv(jnp.dot(h2, up_ref[:, lo_a:lo_a + FF_CHUNK], preferred_element_type=F32), lo_a)
        ug = conv(jnp.dot(h2, up_ref[:, lo_g:lo_g + FF_CHUNK], preferred_element_type=F32), lo_g)
        act = (_silu(ua) * ug).astype(BF16)
        acc = acc + jnp.dot(act, down_ref[lo_a:lo_a + FF_CHUNK, :], preferred_element_type=F32)
    o_ref[0] = _layer_norm(ALPHA * x1 + g2 * acc, g_ref[...], b_ref[...])


def _ffn_call(x1, mod_l, wl, n_ctx_tiles, t_off):
    bsz, rows, d = x1.shape
    nt = rows // TM
    hb = TM // HALO
    n_hblk = rows // HALO
    starts = sorted({max(0 - t_off, 0), max(n_ctx_tiles - t_off, 0)})
    ends = sorted({e for e in (n_ctx_tiles - 1 - t_off, nt - 1) if e >= 0})
    full = lambda a: pl.BlockSpec(a.shape, lambda b, t: (0,) * a.ndim)
    weights = (wl["ffn_up"], wl["conv_w"], wl["conv_b"], wl["ffn_down"], wl["ln2_g"], wl["ln2_b"])
    return pl.pallas_call(
        functools.partial(_ffn_kernel, seq_starts=tuple(starts), seq_ends=tuple(ends)),
        grid=(bsz, nt),
        in_specs=[pl.BlockSpec((1, HALO, d), lambda b, t: (b, jnp.maximum(t * hb - 1, 0), 0)),
                  pl.BlockSpec((1, TM, d), lambda b, t: (b, t, 0)),
                  pl.BlockSpec((1, HALO, d), lambda b, t: (b, jnp.minimum((t + 1) * hb, n_hblk - 1), 0)),
                  pl.BlockSpec((1, 1, 6 * d),
                               lambda b, t: (2 * b + (t + t_off >= n_ctx_tiles).astype(jnp.int32), 0, 0))]
                 + [full(w) for w in weights],
        out_specs=pl.BlockSpec((1, TM, d), lambda b, t: (b, t, 0)),
        out_shape=jax.ShapeDtypeStruct((bsz, rows, d), F32),
        compiler_params=_cparams(("arbitrary", "arbitrary")),
        name="conv_ffn_ln2",
    )(x1, x1, x1, mod_l, *weights)


def _rope_tables(seq, n_ctx):
    pos = jnp.arange(seq, dtype=F32)
    ret_inv = 1.0 / (ROPE_BASE ** jnp.linspace(0.0, 1.0, DK // 2, dtype=F32))
    ret_ang = pos[:, None] * ret_inv
    rc, rs = jnp.cos(ret_ang), jnp.sin(ret_ang)
    ret_cos = jnp.tile(jnp.concatenate([rc, rc], axis=1), (1, N_HEADS))
    ret_sin = jnp.tile(jnp.concatenate([-rs, rs], axis=1), (1, N_HEADS))

    rows = jnp.repeat(jnp.arange(seq // GRID_W, dtype=F32), GRID_W)
    cols = jnp.tile(jnp.arange(GRID_W, dtype=F32), seq // GRID_W)
    n_ax = MLA_D_ROPE // 4
    ax_inv = ROPE_BASE ** (-jnp.arange(n_ax, dtype=F32) / n_ax)
    ra, ca = rows[:, None] * ax_inv, cols[:, None] * ax_inv
    cos32 = jnp.concatenate([jnp.cos(ra), jnp.cos(ra), jnp.cos(ca), jnp.cos(ca)], axis=1)
    sin32 = jnp.concatenate([-jnp.sin(ra), jnp.sin(ra), -jnp.sin(ca), jnp.sin(ca)], axis=1)
    one, zero = jnp.ones((seq, MLA_D_NOPE), F32), jnp.zeros((seq, 32), F32)
    q_cos = jnp.concatenate([one, cos32, zero], axis=1)
    q_sin = jnp.concatenate([0.0 * one, sin32, zero], axis=1)
    misc = jnp.concatenate([cos32, sin32, zero, zero], axis=1)
    lat = jnp.concatenate([ret_cos, ret_sin, q_cos, q_sin, misc], axis=1)

    c1, c0 = jnp.ones((n_ctx, 128), F32), jnp.zeros((n_ctx, 128), F32)
    q_cos_c = jnp.concatenate([jnp.ones((n_ctx, 96), F32), jnp.zeros((n_ctx, 32), F32)], axis=1)
    misc_c = jnp.concatenate([jnp.ones((n_ctx, 32), F32), jnp.zeros((n_ctx, 96), F32)], axis=1)
    ctx = jnp.concatenate([c1, c0, q_cos_c, c0, misc_c], axis=1)
    return jnp.concatenate([ctx, lat], axis=0)


def _swap_idx(n, half):
    j = np.arange(n)
    return (j // (2 * half)) * (2 * half) + (j % (2 * half) + half) % (2 * half)


def _layer_weights(i, w_in, gla_gate_w, gla_gate_b, gla_norm_g, ret_decay, mla_q_norm_g, mla_kv_norm_g,
                   mla_w_uq, mla_w_uk, mla_w_uv, w_out, ln1_g, ln1_b, ffn_up, ffn_conv_w, ffn_conv_b,
                   ffn_down, ln2_g, ln2_b):
    d = D_MODEL
    o = IN_OFFS
    wi = w_in[i]
    sec = lambda k: wi[:, o[k]:o[k + 1]]
    swap_ret = _swap_idx(QK_W, DK // 2)
    swap_kr = _swap_idx(MLA_D_ROPE, MLA_D_ROPE // 4)
    kr = sec(11)
    misc = jnp.concatenate([kr, kr[:, swap_kr], sec(3), jnp.zeros((d, 32), F32)], axis=1)
    w_in_p = jnp.concatenate([sec(0), sec(1), sec(2), sec(4), sec(5), sec(6), sec(7), sec(8), sec(9),
                              sec(10), misc, sec(5)[:, swap_ret], sec(6)[:, swap_ret]], axis=1).astype(BF16)

    wg = jnp.zeros((128, 2 * QK_W), F32)
    wg = wg.at[64:64 + GATE_RANK, 0:QK_W].set(gla_gate_w[i, 0])
    wg = wg.at[64 + GATE_RANK:64 + 2 * GATE_RANK, QK_W:].set(gla_gate_w[i, 1])
    bg = jnp.concatenate([gla_gate_b[i, 0], gla_gate_b[i, 1]])[None, :]

    uq = mla_w_uq[i].reshape(MLA_Q_RANK, MLA_HEADS, MLA_D_NOPE + MLA_D_ROPE)
    z32 = jnp.zeros((MLA_Q_RANK, MLA_HEADS, 32), F32)
    z64 = jnp.zeros((MLA_Q_RANK, MLA_HEADS, 64), F32)
    uq_main = jnp.concatenate([uq, z32], axis=2).reshape(MLA_Q_RANK, -1)
    uq_sw = jnp.concatenate([z64, uq[:, :, MLA_D_NOPE:][:, :, swap_kr], z32], axis=2).reshape(MLA_Q_RANK, -1)
    w_uq = jnp.concatenate([uq_main, uq_sw], axis=1).astype(BF16)

    uk = mla_w_uk[i].reshape(MLA_KV_RANK, MLA_HEADS, MLA_D_NOPE)
    uk_pad = jnp.concatenate([uk, jnp.zeros((MLA_KV_RANK, MLA_HEADS, 64), F32)], axis=2).reshape(MLA_KV_RANK, -1)
    uv = mla_w_uv[i].reshape(MLA_KV_RANK, MLA_HEADS, MLA_DV)
    even = (jnp.arange(MLA_HEADS) % 2 == 0)[None, :, None]
    uv_a = jnp.where(even, uv, 0.0).reshape(MLA_KV_RANK, -1)
    uv_b = jnp.where(even, 0.0, uv).reshape(MLA_KV_RANK, -1)
    place = jnp.zeros((128, MLA_HEADS, MLA_HEAD_PAD), F32)
    eye = jnp.eye(MLA_D_ROPE, dtype=F32)
    place = place.at[0:MLA_D_ROPE, :, MLA_D_NOPE:MLA_D_NOPE + MLA_D_ROPE].set(
        jnp.broadcast_to(eye[:, None, :], (MLA_D_ROPE, MLA_HEADS, MLA_D_ROPE)))
    place = place.reshape(128, -1)
    w_kv = jnp.concatenate([
        jnp.concatenate([uk_pad, uv_a, uv_b], axis=1),
        jnp.concatenate([place, jnp.zeros((128, 2 * MLA_HEADS * MLA_DV), F32)], axis=1)], axis=0).astype(BF16)

    row = lambda a: a[None, :]
    return {
        "w_in": w_in_p, "w_gate": wg.astype(BF16), "b_gate": bg,
        "qn_g": row(mla_q_norm_g[i]), "kvn_g": row(mla_kv_norm_g[i]), "w_uq": w_uq, "w_kv": w_kv,
        "gla_ng": row(jnp.tile(gla_norm_g[i], N_HEADS)),
        "dec_rows": jnp.repeat(ret_decay[i], CHUNK, axis=1)[:, :, None],
        "dec_lanes": jnp.repeat(ret_decay[i], DK, axis=1)[:, None, :],
        "w_out": w_out[i].astype(BF16), "ln1_g": row(ln1_g[i]), "ln1_b": row(ln1_b[i]),
        "ffn_up": ffn_up[i].astype(BF16), "conv_w": ffn_conv_w[i], "conv_b": row(ffn_conv_b[i]),
        "ffn_down": ffn_down[i].astype(BF16), "ln2_g": row(ln2_g[i]), "ln2_b": row(ln2_b[i]),
    }


def kernel(x, c, ctx, c_ctx, ada_w, ada_b, w_in, gla_gate_w, gla_gate_b, gla_norm_g, ret_decay, mla_q_norm_g, mla_kv_norm_g, mla_w_uq, mla_w_uk, mla_w_uv, w_out, ln1_g, ln1_b, ffn_up, ffn_conv_w, ffn_conv_b, ffn_down, ln2_g, ln2_b):
    bsz, seq, d = x.shape
    n_ctx = ctx.shape[1]
    assert d == D_MODEL and seq % TM == 0 and n_ctx % TM == 0 and seq % GRID_W == 0
    n_ctx_tiles = n_ctx // TM
    n_ctx_chunks = n_ctx // CHUNK

    rows = -(-(bsz + 1) // 8) * 8
    c_all = jnp.concatenate([c, c_ctx[None, :], jnp.zeros((rows - bsz - 1, d), F32)], axis=0)
    mod = _ada_call(c_all, ada_w, ada_b)
    mod_ctx = jnp.broadcast_to(mod[:, bsz:bsz + 1], (DEPTH, bsz, 6 * d))
    mod_sel = jnp.stack([mod_ctx, mod[:, :bsz]], axis=2).reshape(DEPTH, 2 * bsz, 1, 6 * d)

    tab = _rope_tables(seq, n_ctx)
    xa = jnp.concatenate([ctx, x], axis=1)

    for i in range(DEPTH):
        wl = _layer_weights(i, w_in, gla_gate_w, gla_gate_b, gla_norm_g, ret_decay, mla_q_norm_g,
                            mla_kv_norm_g, mla_w_uq, mla_w_uk, mla_w_uv, w_out, ln1_g, ln1_b, ffn_up,
                            ffn_conv_w, ffn_conv_b, ffn_down, ln2_g, ln2_b)
        mod_l = mod_sel[i]
        need_ctx = i < DEPTH - 1
        t_off = 0 if need_ctx else n_ctx_tiles
        g1, gv, gg, r1, rv, rg, mq, mkv = _proj_call(xa, mod_l, tab, wl, n_ctx_tiles)
        m_gla = _scan_call(_gla_kernel, "gla_scan", (g1, gv, gg), (wl["gla_ng"],), n_ctx_chunks)
        m_ret = _scan_call(_ret_kernel, "ret_scan", (r1, rv, rg), (wl["dec_rows"], wl["dec_lanes"]),
                           n_ctx_chunks)
        m_mla = _mla_call(mq, mkv, n_ctx, t_off)
        x1 = _outproj_call(xa, m_gla, m_ret, m_mla, mod_l, wl, n_ctx_tiles, t_off)
        xa = _ffn_call(x1, mod_l, wl, n_ctx_tiles, t_off)
    return xa
```

```python
import functools

import numpy as np
import jax
import jax.numpy as jnp
from jax import lax
from jax.experimental import pallas as pl
from jax.experimental.pallas import tpu as pltpu

F32 = jnp.float32
BF16 = jnp.bfloat16

D_MODEL = 1024
DEPTH = 4
GRID_W = 64
N_HEADS = 4
DK = 32
DV = 64
QK_W = N_HEADS * DK
V_W = N_HEADS * DV
GATE_RANK = 16
GLA_TAU = 16.0
MLA_HEADS = 8
MLA_D_NOPE = 64
MLA_D_ROPE = 32
MLA_DV = 64
MLA_Q_RANK = 256
MLA_KV_RANK = 128
MLA_HEAD_PAD = 128
MLA_SCALE = (MLA_D_NOPE + MLA_D_ROPE) ** -0.5
D_FF = 2816
ROPE_BASE = 10000.0
EPS = 1e-6
ALPHA = (2 * DEPTH) ** 0.25
IN_SIZES = (QK_W, QK_W, V_W, 2 * GATE_RANK, V_W, QK_W, QK_W, V_W, V_W,
            MLA_Q_RANK, MLA_KV_RANK, MLA_D_ROPE)
IN_OFFS = tuple(int(o) for o in np.concatenate([[0], np.cumsum(IN_SIZES)]))

P_GQ, P_GK, P_GV, P_GG = 0, 128, 256, 512
P_RQ, P_RK, P_RV, P_RG = 768, 896, 1024, 1280
P_CQ, P_CKV, P_MISC = 1536, 1792, 1920
P_RQS, P_RKS = 2048, 2176
P_W = 2304

TM = 256
SCAN_CHUNK = 256
HALO = 8
FF_CHUNK = 256
VMEM_LIMIT = 56 * 1024 * 1024


def _cparams(sem):
    return pltpu.CompilerParams(dimension_semantics=sem, vmem_limit_bytes=VMEM_LIMIT)


def _iota(shape, dim):
    return lax.broadcasted_iota(jnp.int32, shape, dim)


def _split3(x):
    hi = x.astype(BF16)
    r1 = x - hi.astype(F32)
    mid = r1.astype(BF16)
    lo = (r1 - mid.astype(F32)).astype(BF16)
    return hi, mid, lo


def _dot_exact_rhs01(x, m01):
    hi, mid, lo = _split3(x)
    d = lambda t: jnp.dot(t, m01, preferred_element_type=F32)
    return d(hi) + d(mid) + d(lo)


def _log_sigmoid(z):
    return jnp.minimum(z, 0.0) - jnp.log1p(jnp.exp(-jnp.abs(z)))


def _silu(x):
    return x * jax.nn.sigmoid(x)


def _layer_norm(x, g, b):
    mu = jnp.mean(x, axis=-1, keepdims=True)
    xc = x - mu
    var = jnp.mean(xc * xc, axis=-1, keepdims=True)
    return xc * lax.rsqrt(var + EPS) * g + b


def _rms_norm(x, g):
    return x * lax.rsqrt(jnp.mean(x * x, axis=-1, keepdims=True) + EPS) * g


def _dot_nt(a, b):
    return lax.dot_general(a, b, (((1,), (1,)), ((), ())), preferred_element_type=F32)


def _dot_tn(a, b):
    return lax.dot_general(a, b, (((0,), (0,)), ((), ())), preferred_element_type=F32)


def _ada_kernel(c_ref, w_ref, b_ref, o_ref):
    a = _silu(c_ref[...]).astype(BF16)
    o_ref[0] = jnp.dot(a, w_ref[0].astype(BF16), preferred_element_type=F32) + b_ref[0]


def _ada_call(c_all, ada_w, ada_b):
    depth, d, e = ada_w.shape
    rows = c_all.shape[0]
    bn = 1024
    return pl.pallas_call(
        _ada_kernel,
        grid=(depth, e // bn),
        in_specs=[pl.BlockSpec((rows, d), lambda l, n: (0, 0)),
                  pl.BlockSpec((1, d, bn), lambda l, n: (l, 0, n)),
                  pl.BlockSpec((1, 1, bn), lambda l, n: (l, 0, n))],
        out_specs=pl.BlockSpec((1, rows, bn), lambda l, n: (l, 0, n)),
        out_shape=jax.ShapeDtypeStruct((depth, rows, e), F32),
        compiler_params=_cparams(("arbitrary", "arbitrary")),
        name="ada_mod",
    )(c_all, ada_w, ada_b.reshape(depth, 1, e))


def _proj_kernel(x_ref, mod_ref, tab_ref, w_in_ref, w_gate_ref, b_gate_ref, qn_g_ref, kvn_g_ref,
                 w_uq_ref, w_kv_ref,
                 g1_ref, gv_ref, gg_ref, r1_ref, rv_ref, rg_ref, mq_ref, mkv_ref):
    d = D_MODEL
    x = x_ref[0]
    mod = mod_ref[0]
    h = (x * (1.0 + mod[:, d:2 * d]) + mod[:, 0:d]).astype(BF16)
    p = jnp.dot(h, w_in_ref[...], preferred_element_type=F32)

    ret_cos = tab_ref[:, 0:128]
    ret_sin = tab_ref[:, 128:256]
    q_cos = tab_ref[:, 256:384]
    q_sin = tab_ref[:, 384:512]
    misc_tab = tab_ref[:, 512:640]

    misc = p[:, P_MISC:P_MISC + 128]
    z = jnp.dot(misc.astype(BF16), w_gate_ref[...], preferred_element_type=F32) + b_gate_ref[...]
    log_a = _log_sigmoid(z) / GLA_TAU
    g1_ref[0, :, 0:128] = p[:, P_GQ:P_GQ + 128] * (DK ** -0.5)
    g1_ref[0, :, 128:256] = p[:, P_GK:P_GK + 128]
    g1_ref[0, :, 256:512] = log_a
    gv_ref[0] = p[:, P_GV:P_GV + V_W].astype(BF16)
    gg_ref[0] = p[:, P_GG:P_GG + V_W]

    r1_ref[0, :, 0:128] = p[:, P_RQ:P_RQ + 128] * ret_cos + p[:, P_RQS:P_RQS + 128] * ret_sin
    ks = DK ** -0.5
    r1_ref[0, :, 128:256] = ((p[:, P_RK:P_RK + 128] * ks) * ret_cos
                             + (p[:, P_RKS:P_RKS + 128] * ks) * ret_sin)
    rv_ref[0] = p[:, P_RV:P_RV + V_W].astype(BF16)
    rg_ref[0] = p[:, P_RG:P_RG + V_W]

    cq = _rms_norm(p[:, P_CQ:P_CQ + MLA_Q_RANK], qn_g_ref[...]).astype(BF16)
    qq = jnp.dot(cq, w_uq_ref[...], preferred_element_type=F32)
    qw = MLA_HEADS * MLA_HEAD_PAD
    for hh in range(MLA_HEADS):
        lo = hh * MLA_HEAD_PAD
        mq_ref[0, :, lo:lo + MLA_HEAD_PAD] = (
            qq[:, lo:lo + MLA_HEAD_PAD] * q_cos + qq[:, qw + lo:qw + lo + MLA_HEAD_PAD] * q_sin
        ).astype(BF16)

    ckv = _rms_norm(p[:, P_CKV:P_CKV + MLA_KV_RANK], kvn_g_ref[...])
    prod = misc * misc_tab
    kr = prod + pltpu.roll(prod, 128 - MLA_D_ROPE, axis=1)
    lhs = jnp.concatenate([ckv, kr], axis=1).astype(BF16)
    mkv_ref[0] = jnp.dot(lhs, w_kv_ref[...], preferred_element_type=F32).astype(BF16)


def _proj_call(xa, mod_l, tab, wl, n_ctx_tiles):
    bsz, t_all, d = xa.shape
    nt = t_all // TM
    tile = lambda w: pl.BlockSpec((1, TM, w), lambda b, t: (b, t, 0))
    full = lambda a: pl.BlockSpec(a.shape, lambda b, t: (0,) * a.ndim)
    sds = lambda w, dt: jax.ShapeDtypeStruct((bsz, t_all, w), dt)
    weights = (wl["w_in"], wl["w_gate"], wl["b_gate"], wl["qn_g"], wl["kvn_g"], wl["w_uq"], wl["w_kv"])
    return pl.pallas_call(
        _proj_kernel,
        grid=(bsz, nt),
        in_specs=[tile(d),
                  pl.BlockSpec((1, 1, 6 * d), lambda b, t: (2 * b + (t >= n_ctx_tiles).astype(jnp.int32), 0, 0)),
                  pl.BlockSpec((TM, tab.shape[1]), lambda b, t: (t, 0))]
                 + [full(w) for w in weights],
        out_specs=[tile(512), tile(V_W), tile(V_W), tile(256), tile(V_W), tile(V_W),
                   tile(MLA_HEADS * MLA_HEAD_PAD), tile(2048)],
        out_shape=[sds(512, F32), sds(V_W, BF16), sds(V_W, F32), sds(256, F32), sds(V_W, BF16),
                   sds(V_W, F32), sds(MLA_HEADS * MLA_HEAD_PAD, BF16), sds(2048, BF16)],
        compiler_params=_cparams(("arbitrary", "arbitrary")),
        name="proj",
    )(xa, mod_l, tab, *weights)


def _key_lane_masks():
    lane_head = _iota((1, QK_W), 1) // DK
    return [jnp.where(lane_head == h, 1.0, 0.0).astype(BF16) for h in range(N_HEADS)]


def _state_mask():
    return jnp.where((_iota((V_W, QK_W), 0) // DV) == (_iota((V_W, QK_W), 1) // DK), 1.0, 0.0)


def _fill_head_values(v_ref, v4_ref, n_rows):
    lane_head = _iota((TM, V_W), 1) // DV

    def body(i, carry):
        r0 = pl.multiple_of(i * TM, TM)
        v = v_ref[0, pl.ds(r0, TM), :].astype(F32)
        for h in range(N_HEADS):
            v4_ref[h, pl.ds(r0, TM), :] = jnp.where(lane_head == h, v, 0.0).astype(BF16)
        return carry
    lax.fori_loop(0, n_rows // TM, body, 0)


def _chunk_step(qe, ke, weights, qdec, kdec, sdec, s, v_ref, v4_ref, r0, key_masks, state_mask):
    c = SCAN_CHUNK
    o = None
    for h in range(N_HEADS):
        att = _dot_nt(qe * key_masks[h], ke) * weights[h]
        part = jnp.dot(att.astype(BF16), v4_ref[h, pl.ds(r0, c), :], preferred_element_type=F32)
        o = part if o is None else o + part
    o = o + _dot_nt(qdec, s.astype(BF16))
    upd = _dot_tn(v_ref[0, pl.ds(r0, c), :], kdec)
    return o, s * sdec + upd * state_mask


def _run_scan(chunk_fn, sf_ref, sb_ref, of_ref, ob_ref, n_ctx_chunks, n_chunks):
    sf_ref[...] = jnp.zeros_like(sf_ref)
    sb_ref[...] = jnp.zeros_like(sb_ref)

    def body(i, carry):
        rf = pl.multiple_of(i * SCAN_CHUNK, SCAN_CHUNK)
        cb = jnp.where(i < n_ctx_chunks, n_ctx_chunks - 1 - i, n_chunks + n_ctx_chunks - 1 - i)
        rb = pl.multiple_of(cb * SCAN_CHUNK, SCAN_CHUNK)
        o_f, s_f = chunk_fn(rf, True, sf_ref[...])
        o_b, s_b = chunk_fn(rb, False, sb_ref[...])
        of_ref[pl.ds(rf, SCAN_CHUNK), :] = o_f
        ob_ref[pl.ds(rb, SCAN_CHUNK), :] = o_b
        sf_ref[...] = s_f
        sb_ref[...] = s_b
        return carry
    lax.fori_loop(0, n_chunks, body, 0)


def _group_ones():
    same = (_iota((V_W, V_W), 0) // DV) == (_iota((V_W, V_W), 1) // DV)
    return jnp.where(same, 1.0, 0.0).astype(BF16)


def _gla_kernel(g1_ref, gv_ref, gg_ref, ng_ref, out_ref, of_ref, ob_ref, sf_ref, sb_ref, v4_ref, *,
                n_ctx_chunks, n_chunks):
    c = SCAN_CHUNK
    key_masks = _key_lane_masks()
    state_mask = _state_mask()
    ti, si = _iota((c, c), 0), _iota((c, c), 1)
    causal = {True: jnp.where(si <= ti, 1.0, 0.0), False: jnp.where(si >= ti, 1.0, 0.0)}
    tri = {fwd: m.astype(BF16) for fwd, m in causal.items()}
    _fill_head_values(gv_ref, v4_ref, n_chunks * c)

    def chunk_fn(r0, fwd, s):
        q = g1_ref[0, pl.ds(r0, c), 0:128]
        k = g1_ref[0, pl.ds(r0, c), 128:256]
        a = g1_ref[0, pl.ds(r0, c), 256:384] if fwd else g1_ref[0, pl.ds(r0, c), 384:512]
        a_hi = a.astype(BF16)
        a_lo = (a - a_hi.astype(F32)).astype(BF16)
        b = (jnp.dot(tri[fwd], a_hi, preferred_element_type=F32)
             + jnp.dot(tri[fwd], a_lo, preferred_element_type=F32))
        b_mid = b[c // 2:c // 2 + 1, :]
        b_end = b[c - 1:c, :] if fwd else b[0:1, :]
        qe = (q * jnp.exp(b - b_mid)).astype(BF16)
        ke = (k * jnp.exp(b_mid - b)).astype(BF16)
        qdec = (q * jnp.exp(b)).astype(BF16)
        kdec = (k * jnp.exp(b_end - b)).astype(BF16)
        return _chunk_step(qe, ke, [causal[fwd]] * N_HEADS, qdec, kdec, jnp.exp(b_end), s,
                           gv_ref, v4_ref, r0, key_masks, state_mask)

    _run_scan(chunk_fn, sf_ref, sb_ref, of_ref, ob_ref, n_ctx_chunks, n_chunks)

    ones_g = _group_ones()
    ng = ng_ref[...]

    def norm_body(i, carry):
        r0 = pl.multiple_of(i * TM, TM)
        o = of_ref[pl.ds(r0, TM), :] + ob_ref[pl.ds(r0, TM), :]
        ms = _dot_exact_rhs01(o * o, ones_g) * (1.0 / DV)
        y = o * lax.rsqrt(ms + EPS) * ng * _silu(gg_ref[0, pl.ds(r0, TM), :])
        out_ref[0, pl.ds(r0, TM), :] = y.astype(BF16)
        return carry
    lax.fori_loop(0, (n_chunks * c) // TM, norm_body, 0)


def _ret_kernel(r1_ref, rv_ref, rg_ref, dec_head_ref, dec_lanes_ref, out_ref, of_ref, ob_ref, sf_ref,
                sb_ref, v4_ref, dm_ref, *, n_ctx_chunks, n_chunks):
    c = SCAN_CHUNK
    key_masks = _key_lane_masks()
    state_mask = _state_mask()
    ti, si = _iota((c, c), 0), _iota((c, c), 1)
    tl = _iota((c, QK_W), 0).astype(F32)
    _fill_head_values(rv_ref, v4_ref, n_chunks * c)

    consts = []
    for di in range(2):
        fwd = di == 0
        rel = (ti - si) if fwd else (si - ti)
        relf = jnp.maximum(rel, 0).astype(F32)
        for h in range(N_HEADS):
            lg = _log_sigmoid(dec_head_ref[di, h])
            dm_ref[di, h] = jnp.where(rel >= 0, jnp.exp(relf * lg), 0.0)
        lg_lanes = _log_sigmoid(dec_lanes_ref[di])
        n = tl if fwd else (c - 1.0) - tl
        consts.append((jnp.exp((n + 1.0) * lg_lanes), jnp.exp((c - 1.0 - n) * lg_lanes),
                       jnp.exp(float(c) * lg_lanes)))

    def chunk_fn(r0, fwd, s):
        di = 0 if fwd else 1
        q_dec, k_dec, s_dec = consts[di]
        q = r1_ref[0, pl.ds(r0, c), 0:128]
        k = r1_ref[0, pl.ds(r0, c), 128:256]
        return _chunk_step(q.astype(BF16), k.astype(BF16), [dm_ref[di, h] for h in range(N_HEADS)],
                           (q * q_dec).astype(BF16), (k * k_dec).astype(BF16), s_dec, s,
                           rv_ref, v4_ref, r0, key_masks, state_mask)

    _run_scan(chunk_fn, sf_ref, sb_ref, of_ref, ob_ref, n_ctx_chunks, n_chunks)

    ones_g = _group_ones()

    def norm_body(i, carry):
        r0 = pl.multiple_of(i * TM, TM)
        o = of_ref[pl.ds(r0, TM), :] + ob_ref[pl.ds(r0, TM), :]
        mu = _dot_exact_rhs01(o, ones_g) * (1.0 / DV)
        oc = o - mu
        var = _dot_exact_rhs01(oc * oc, ones_g) * (1.0 / DV)
        y = oc * lax.rsqrt(var + EPS) * _silu(rg_ref[0, pl.ds(r0, TM), :])
        out_ref[0, pl.ds(r0, TM), :] = y.astype(BF16)
        return carry
    lax.fori_loop(0, (n_chunks * c) // TM, norm_body, 0)


def _scan_call(kernel, name, seq_inputs, small_inputs, extra_scratch, n_ctx):
    bsz, t_all, _ = seq_inputs[0].shape
    assert t_all % SCAN_CHUNK == 0 and n_ctx % SCAN_CHUNK == 0
    per_b = lambda a: pl.BlockSpec((1,) + a.shape[1:], lambda b: (b, 0, 0))
    full = lambda a: pl.BlockSpec(a.shape, lambda b: (0,) * a.ndim)
    state = pltpu.VMEM((V_W, QK_W), F32)
    o_dir = pltpu.VMEM((t_all, V_W), F32)
    return pl.pallas_call(
        functools.partial(kernel, n_ctx_chunks=n_ctx // SCAN_CHUNK, n_chunks=t_all // SCAN_CHUNK),
        grid=(bsz,),
        in_specs=[per_b(a) for a in seq_inputs] + [full(a) for a in small_inputs],
        out_specs=pl.BlockSpec((1, t_all, V_W), lambda b: (b, 0, 0)),
        out_shape=jax.ShapeDtypeStruct((bsz, t_all, V_W), BF16),
        scratch_shapes=[o_dir, o_dir, state, state, pltpu.VMEM((N_HEADS, t_all, V_W), BF16)] + extra_scratch,
        compiler_params=_cparams(("arbitrary",)),
        name=name,
    )(*seq_inputs, *small_inputs)


def _mla_kernel(q_ref, kv_ref, o_ref, *, n_ctx, t_off):
    t = pl.program_id(1) + t_off
    kw = MLA_HEADS * MLA_HEAD_PAD
    vw = MLA_HEADS * MLA_DV

    def attend(n_keys):
        for pair in range(MLA_HEADS // 2):
            acc = None
            for hh in range(2):
                lo = (2 * pair + hh) * MLA_HEAD_PAD
                s = _dot_nt(q_ref[0, :, lo:lo + MLA_HEAD_PAD], kv_ref[0, 0:n_keys, lo:lo + MLA_HEAD_PAD])
                s = s * MLA_SCALE
                e = jnp.exp(s - jnp.max(s, axis=-1, keepdims=True))
                p = e * (1.0 / jnp.sum(e, axis=-1, keepdims=True))
                vlo = kw + hh * vw + pair * 128
                pv = jnp.dot(p.astype(BF16), kv_ref[0, 0:n_keys, vlo:vlo + 128], preferred_element_type=F32)
                acc = pv if acc is None else acc + pv
            o_ref[0, :, pair * 128:(pair + 1) * 128] = acc.astype(BF16)

    n_ctx_tiles = n_ctx // TM

    @pl.when(t < n_ctx_tiles)
    def _():
        attend(n_ctx)

    @pl.when(t >= n_ctx_tiles)
    def _():
        attend(kv_ref.shape[1])


def _mla_call(mq, mkv, n_ctx, t_off):
    bsz, t_all, qw = mq.shape
    nt = t_all // TM - t_off
    return pl.pallas_call(
        functools.partial(_mla_kernel, n_ctx=n_ctx, t_off=t_off),
        grid=(bsz, nt),
        in_specs=[pl.BlockSpec((1, TM, qw), lambda b, t: (b, t + t_off, 0)),
                  pl.BlockSpec((1, t_all, mkv.shape[2]), lambda b, t: (b, 0, 0))],
        out_specs=pl.BlockSpec((1, TM, MLA_HEADS * MLA_DV), lambda b, t: (b, t, 0)),
        out_shape=jax.ShapeDtypeStruct((bsz, nt * TM, MLA_HEADS * MLA_DV), BF16),
        compiler_params=_cparams(("arbitrary", "arbitrary")),
        name="mla_attn",
    )(mq, mkv)


def _outproj_kernel(x_ref, mg_ref, mr_ref, mm_ref, mod_ref, w_ref, g_ref, b_ref, o_ref):
    d = D_MODEL
    w = V_W
    y = (jnp.dot(mg_ref[0], w_ref[0:w, :], preferred_element_type=F32)
         + jnp.dot(mr_ref[0], w_ref[w:2 * w, :], preferred_element_type=F32)
         + jnp.dot(mm_ref[0], w_ref[2 * w:, :], preferred_element_type=F32))
    g1 = mod_ref[0][:, 2 * d:3 * d]
    o_ref[0] = _layer_norm(ALPHA * x_ref[0] + g1 * y, g_ref[...], b_ref[...])


def _outproj_call(xa, m_gla, m_ret, m_mla, mod_l, wl, n_ctx_tiles, t_off):
    bsz, t_all, d = xa.shape
    nt = t_all // TM - t_off
    src = lambda w: pl.BlockSpec((1, TM, w), lambda b, t: (b, t + t_off, 0))
    full = lambda a: pl.BlockSpec(a.shape, lambda b, t: (0,) * a.ndim)
    return pl.pallas_call(
        _outproj_kernel,
        grid=(bsz, nt),
        in_specs=[src(d), src(V_W), src(V_W),
                  pl.BlockSpec((1, TM, m_mla.shape[2]), lambda b, t: (b, t, 0)),
                  pl.BlockSpec((1, 1, 6 * d),
                               lambda b, t: (2 * b + (t + t_off >= n_ctx_tiles).astype(jnp.int32), 0, 0)),
                  full(wl["w_out"]), full(wl["ln1_g"]), full(wl["ln1_b"])],
        out_specs=pl.BlockSpec((1, TM, d), lambda b, t: (b, t, 0)),
        out_shape=jax.ShapeDtypeStruct((bsz, nt * TM, d), F32),
        compiler_params=_cparams(("arbitrary", "arbitrary")),
        name="out_proj_ln1",
    )(xa, m_gla, m_ret, m_mla, mod_l, wl["w_out"], wl["ln1_g"], wl["ln1_b"])


def _ffn_kernel(xp_ref, x_ref, xn_ref, mod_ref, up_ref, cw_ref, cb_ref, down_ref, g_ref, b_ref, o_ref,
                h2_ref, ua_ref, ug_ref, act_ref, *, seq_starts, seq_ends):
    d = D_MODEL
    n_rows = TM + 2 * HALO
    n_ff = D_FF // FF_CHUNK
    t = pl.program_id(1)
    mod = mod_ref[0]
    sh2, sc2, g2 = mod[:, 3 * d:4 * d], mod[:, 4 * d:5 * d], mod[:, 5 * d:6 * d]
    x1 = x_ref[0]
    h2_ref[0:HALO, :] = (xp_ref[0] * (1.0 + sc2) + sh2).astype(BF16)
    h2_ref[HALO:HALO + TM, :] = (x1 * (1.0 + sc2) + sh2).astype(BF16)
    h2_ref[HALO + TM:n_rows, :] = (xn_ref[0] * (1.0 + sc2) + sh2).astype(BF16)

    is_start = functools.reduce(jnp.logical_or, [t == s for s in seq_starts])
    is_end = functools.reduce(jnp.logical_or, [t == e for e in seq_ends])
    keep_prev = jnp.where(is_start, 0.0, 1.0)
    keep_next = jnp.where(is_end, 0.0, 1.0)

    def up_stage(j):
        slot = j % 2
        for u_ref, lo in ((ua_ref, j * FF_CHUNK), (ug_ref, D_FF + j * FF_CHUNK)):
            u = jnp.dot(h2_ref[...], up_ref[:, lo:lo + FF_CHUNK], preferred_element_type=F32)
            u_ref[slot, 0:HALO, :] = u[0:HALO] * keep_prev
            u_ref[slot, HALO:HALO + TM, :] = u[HALO:HALO + TM]
            u_ref[slot, HALO + TM:n_rows, :] = u[HALO + TM:n_rows] * keep_next

    def conv(u_ref, slot, lo):
        cw = cw_ref[:, lo:lo + FF_CHUNK]
        return (u_ref[slot, HALO - 1:HALO - 1 + TM, :] * cw[0:1]
                + u_ref[slot, HALO:HALO + TM, :] * cw[1:2]
                + u_ref[slot, HALO + 1:HALO + 1 + TM, :] * cw[2:3]
                + cb_ref[:, lo:lo + FF_CHUNK])

    def act_stage(j):
        slot = j % 2
        a = conv(ua_ref, slot, j * FF_CHUNK)
        gt = conv(ug_ref, slot, D_FF + j * FF_CHUNK)
        act_ref[slot] = (_silu(a) * gt).astype(BF16)

    acc = None
    up_stage(0)
    for j in range(n_ff):
        if j + 1 < n_ff:
            up_stage(j + 1)
        act_stage(j)
        part = jnp.dot(act_ref[j % 2], down_ref[j * FF_CHUNK:(j + 1) * FF_CHUNK, :],
                       preferred_element_type=F32)
        acc = part if acc is None else acc + part
    o_ref[0] = _layer_norm(ALPHA * x1 + g2 * acc, g_ref[...], b_ref[...])


def _ffn_call(x1, mod_l, wl, n_ctx_tiles, t_off):
    bsz, rows, d = x1.shape
    nt = rows // TM
    hb = TM // HALO
    n_hblk = rows // HALO
    starts = sorted({max(0 - t_off, 0), max(n_ctx_tiles - t_off, 0)})
    ends = sorted({e for e in (n_ctx_tiles - 1 - t_off, nt - 1) if e >= 0})
    full = lambda a: pl.BlockSpec(a.shape, lambda b, t: (0,) * a.ndim)
    weights = (wl["ffn_up"], wl["conv_w"], wl["conv_b"], wl["ffn_down"], wl["ln2_g"], wl["ln2_b"])
    n_rows = TM + 2 * HALO
    return pl.pallas_call(
        functools.partial(_ffn_kernel, seq_starts=tuple(starts), seq_ends=tuple(ends)),
        grid=(bsz, nt),
        in_specs=[pl.BlockSpec((1, HALO, d), lambda b, t: (b, jnp.maximum(t * hb - 1, 0), 0)),
                  pl.BlockSpec((1, TM, d), lambda b, t: (b, t, 0)),
                  pl.BlockSpec((1, HALO, d), lambda b, t: (b, jnp.minimum((t + 1) * hb, n_hblk - 1), 0)),
                  pl.BlockSpec((1, 1, 6 * d),
                               lambda b, t: (2 * b + (t + t_off >= n_ctx_tiles).astype(jnp.int32), 0, 0))]
                 + [full(w) for w in weights],
        out_specs=pl.BlockSpec((1, TM, d), lambda b, t: (b, t, 0)),
        out_shape=jax.ShapeDtypeStruct((bsz, rows, d), F32),
        scratch_shapes=[pltpu.VMEM((n_rows, d), BF16),
                        pltpu.VMEM((2, n_rows, FF_CHUNK), F32), pltpu.VMEM((2, n_rows, FF_CHUNK), F32),
                        pltpu.VMEM((2, TM, FF_CHUNK), BF16)],
        compiler_params=_cparams(("arbitrary", "arbitrary")),
        name="conv_ffn_ln2",
    )(x1, x1, x1, mod_l, *weights)


def _rope_tables(seq, n_ctx):
    pos = jnp.arange(seq, dtype=F32)
    ret_inv = 1.0 / (ROPE_BASE ** jnp.linspace(0.0, 1.0, DK // 2, dtype=F32))
    ret_ang = pos[:, None] * ret_inv
    rc, rs = jnp.cos(ret_ang), jnp.sin(ret_ang)
    ret_cos = jnp.tile(jnp.concatenate([rc, rc], axis=1), (1, N_HEADS))
    ret_sin = jnp.tile(jnp.concatenate([-rs, rs], axis=1), (1, N_HEADS))

    rows = jnp.repeat(jnp.arange(seq // GRID_W, dtype=F32), GRID_W)
    cols = jnp.tile(jnp.arange(GRID_W, dtype=F32), seq // GRID_W)
    n_ax = MLA_D_ROPE // 4
    ax_inv = ROPE_BASE ** (-jnp.arange(n_ax, dtype=F32) / n_ax)
    ra, ca = rows[:, None] * ax_inv, cols[:, None] * ax_inv
    cos32 = jnp.concatenate([jnp.cos(ra), jnp.cos(ra), jnp.cos(ca), jnp.cos(ca)], axis=1)
    sin32 = jnp.concatenate([-jnp.sin(ra), jnp.sin(ra), -jnp.sin(ca), jnp.sin(ca)], axis=1)
    one, zero = jnp.ones((seq, MLA_D_NOPE), F32), jnp.zeros((seq, 32), F32)
    q_cos = jnp.concatenate([one, cos32, zero], axis=1)
    q_sin = jnp.concatenate([0.0 * one, sin32, zero], axis=1)
    misc = jnp.concatenate([cos32, sin32, zero, zero], axis=1)
    lat = jnp.concatenate([ret_cos, ret_sin, q_cos, q_sin, misc], axis=1)

    c1, c0 = jnp.ones((n_ctx, 128), F32), jnp.zeros((n_ctx, 128), F32)
    q_cos_c = jnp.concatenate([jnp.ones((n_ctx, 96), F32), jnp.zeros((n_ctx, 32), F32)], axis=1)
    misc_c = jnp.concatenate([jnp.ones((n_ctx, 32), F32), jnp.zeros((n_ctx, 96), F32)], axis=1)
    ctx = jnp.concatenate([c1, c0, q_cos_c, c0, misc_c], axis=1)
    return jnp.concatenate([ctx, lat], axis=0)


def _swap_idx(n, half):
    j = np.arange(n)
    return (j // (2 * half)) * (2 * half) + (j % (2 * half) + half) % (2 * half)


def _layer_weights(i, w_in, gla_gate_w, gla_gate_b, gla_norm_g, ret_decay, mla_q_norm_g, mla_kv_norm_g,
                   mla_w_uq, mla_w_uk, mla_w_uv, w_out, ln1_g, ln1_b, ffn_up, ffn_conv_w, ffn_conv_b,
                   ffn_down, ln2_g, ln2_b):
    d = D_MODEL
    o = IN_OFFS
    wi = w_in[i]
    sec = lambda k: wi[:, o[k]:o[k + 1]]
    swap_ret = _swap_idx(QK_W, DK // 2)
    swap_kr = _swap_idx(MLA_D_ROPE, MLA_D_ROPE // 4)
    kr = sec(11)
    misc = jnp.concatenate([kr, kr[:, swap_kr], sec(3), jnp.zeros((d, 32), F32)], axis=1)
    w_in_p = jnp.concatenate([sec(0), sec(1), sec(2), sec(4), sec(5), sec(6), sec(7), sec(8), sec(9),
                              sec(10), misc, sec(5)[:, swap_ret], sec(6)[:, swap_ret]], axis=1).astype(BF16)

    wg = jnp.zeros((128, 2 * QK_W), F32)
    wg = wg.at[64:64 + GATE_RANK, 0:QK_W].set(gla_gate_w[i, 0])
    wg = wg.at[64 + GATE_RANK:64 + 2 * GATE_RANK, QK_W:].set(gla_gate_w[i, 1])
    bg = jnp.concatenate([gla_gate_b[i, 0], gla_gate_b[i, 1]])[None, :]

    uq = mla_w_uq[i].reshape(MLA_Q_RANK, MLA_HEADS, MLA_D_NOPE + MLA_D_ROPE)
    z32 = jnp.zeros((MLA_Q_RANK, MLA_HEADS, 32), F32)
    z64 = jnp.zeros((MLA_Q_RANK, MLA_HEADS, 64), F32)
    uq_main = jnp.concatenate([uq, z32], axis=2).reshape(MLA_Q_RANK, -1)
    uq_sw = jnp.concatenate([z64, uq[:, :, MLA_D_NOPE:][:, :, swap_kr], z32], axis=2).reshape(MLA_Q_RANK, -1)
    w_uq = jnp.concatenate([uq_main, uq_sw], axis=1).astype(BF16)

    uk = mla_w_uk[i].reshape(MLA_KV_RANK, MLA_HEADS, MLA_D_NOPE)
    uk_pad = jnp.concatenate([uk, jnp.zeros((MLA_KV_RANK, MLA_HEADS, 64), F32)], axis=2).reshape(MLA_KV_RANK, -1)
    uv = mla_w_uv[i].reshape(MLA_KV_RANK, MLA_HEADS, MLA_DV)
    even = (jnp.arange(MLA_HEADS) % 2 == 0)[None, :, None]
    uv_a = jnp.where(even, uv, 0.0).reshape(MLA_KV_RANK, -1)
    uv_b = jnp.where(even, 0.0, uv).reshape(MLA_KV_RANK, -1)
    place = jnp.zeros((128, MLA_HEADS, MLA_HEAD_PAD), F32)
    eye = jnp.eye(MLA_D_ROPE, dtype=F32)
    place = place.at[0:MLA_D_ROPE, :, MLA_D_NOPE:MLA_D_NOPE + MLA_D_ROPE].set(
        jnp.broadcast_to(eye[:, None, :], (MLA_D_ROPE, MLA_HEADS, MLA_D_ROPE)))
    place = place.reshape(128, -1)
    w_kv = jnp.concatenate([
        jnp.concatenate([uk_pad, uv_a, uv_b], axis=1),
        jnp.concatenate([place, jnp.zeros((128, 2 * MLA_HEADS * MLA_DV), F32)], axis=1)], axis=0).astype(BF16)

    row = lambda a: a[None, :]
    return {
        "w_in": w_in_p, "w_gate": wg.astype(BF16), "b_gate": bg,
        "qn_g": row(mla_q_norm_g[i]), "kvn_g": row(mla_kv_norm_g[i]), "w_uq": w_uq, "w_kv": w_kv,
        "gla_ng": row(jnp.tile(gla_norm_g[i], N_HEADS)),
        "dec_head": jnp.broadcast_to(ret_decay[i][:, :, None, None], (2, N_HEADS, 1, SCAN_CHUNK)),
        "dec_lanes": jnp.repeat(ret_decay[i], DK, axis=1)[:, None, :],
        "w_out": w_out[i].astype(BF16), "ln1_g": row(ln1_g[i]), "ln1_b": row(ln1_b[i]),
        "ffn_up": ffn_up[i].astype(BF16), "conv_w": ffn_conv_w[i], "conv_b": row(ffn_conv_b[i]),
        "ffn_down": ffn_down[i].astype(BF16), "ln2_g": row(ln2_g[i]), "ln2_b": row(ln2_b[i]),
    }


def kernel(x, c, ctx, c_ctx, ada_w, ada_b, w_in, gla_gate_w, gla_gate_b, gla_norm_g, ret_decay, mla_q_norm_g, mla_kv_norm_g, mla_w_uq, mla_w_uk, mla_w_uv, w_out, ln1_g, ln1_b, ffn_up, ffn_conv_w, ffn_conv_b, ffn_down, ln2_g, ln2_b):
    bsz, seq, d = x.shape
    n_ctx = ctx.shape[1]
    assert d == D_MODEL and seq % TM == 0 and n_ctx % TM == 0 and seq % GRID_W == 0
    n_ctx_tiles = n_ctx // TM

    rows = -(-(bsz + 1) // 8) * 8
    c_all = jnp.concatenate([c, c_ctx[None, :], jnp.zeros((rows - bsz - 1, d), F32)], axis=0)
    mod = _ada_call(c_all, ada_w, ada_b)
    mod_ctx = jnp.broadcast_to(mod[:, bsz:bsz + 1], (DEPTH, bsz, 6 * d))
    mod_sel = jnp.stack([mod_ctx, mod[:, :bsz]], axis=2).reshape(DEPTH, 2 * bsz, 1, 6 * d)

    tab = _rope_tables(seq, n_ctx)
    xa = jnp.concatenate([ctx, x], axis=1)
    dmat_scratch = [pltpu.VMEM((2, N_HEADS, SCAN_CHUNK, SCAN_CHUNK), F32)]

    for i in range(DEPTH):
        wl = _layer_weights(i, w_in, gla_gate_w, gla_gate_b, gla_norm_g, ret_decay, mla_q_norm_g,
                            mla_kv_norm_g, mla_w_uq, mla_w_uk, mla_w_uv, w_out, ln1_g, ln1_b, ffn_up,
                            ffn_conv_w, ffn_conv_b, ffn_down, ln2_g, ln2_b)
        mod_l = mod_sel[i]
        need_ctx = i < DEPTH - 1
        t_off = 0 if need_ctx else n_ctx_tiles
        g1, gv, gg, r1, rv, rg, mq, mkv = _proj_call(xa, mod_l, tab, wl, n_ctx_tiles)
        m_gla = _scan_call(_gla_kernel, "gla_scan", (g1, gv, gg), (wl["gla_ng"],), [], n_ctx)
        m_ret = _scan_call(_ret_kernel, "ret_scan", (r1, rv, rg), (wl["dec_head"], wl["dec_lanes"]),
                           dmat_scratch, n_ctx)
        m_mla = _mla_call(mq, mkv, n_ctx, t_off)
        x1 = _outproj_call(xa, m_gla, m_ret, m_mla, mod_l, wl, n_ctx_tiles, t_off)
        xa = _ffn_call(x1, mod_l, wl, n_ctx_tiles, t_off)
    return xa
```

```python
import functools

import numpy as np
import jax
import jax.numpy as jnp
from jax import lax
from jax.experimental import pallas as pl
from jax.experimental.pallas import tpu as pltpu

F32 = jnp.float32
BF16 = jnp.bfloat16

D_MODEL = 1024
DEPTH = 4
GRID_W = 64
N_HEADS = 4
DK = 32
DV = 64
QK_W = N_HEADS * DK
V_W = N_HEADS * DV
GATE_RANK = 16
GLA_TAU = 16.0
MLA_HEADS = 8
MLA_D_NOPE = 64
MLA_D_ROPE = 32
MLA_DV = 64
MLA_Q_RANK = 256
MLA_KV_RANK = 128
MLA_HEAD_PAD = 128
MLA_SCALE = (MLA_D_NOPE + MLA_D_ROPE) ** -0.5
MLA_QSCALE = MLA_SCALE * float(np.log2(np.e))
D_FF = 2816
ROPE_BASE = 10000.0
EPS = 1e-6
ALPHA = (2 * DEPTH) ** 0.25
IN_SIZES = (QK_W, QK_W, V_W, 2 * GATE_RANK, V_W, QK_W, QK_W, V_W, V_W,
            MLA_Q_RANK, MLA_KV_RANK, MLA_D_ROPE)
IN_OFFS = tuple(int(o) for o in np.concatenate([[0], np.cumsum(IN_SIZES)]))

P_GQ, P_GK, P_GV, P_GG = 0, 128, 256, 512
P_RQ, P_RK, P_RV, P_RG = 768, 896, 1024, 1280
P_CQ, P_CKV, P_MISC = 1536, 1792, 1920
P_RQS, P_RKS = 2048, 2176
P_W = 2304

TM = 256
SCAN_CHUNK = 256
GLA_MAX_CHUNK_LOG_DECAY = 64.0
HALO = 8
FF_CHUNK = 256
VMEM_LIMIT = 56 * 1024 * 1024


def _cparams(sem):
    return pltpu.CompilerParams(dimension_semantics=sem, vmem_limit_bytes=VMEM_LIMIT)


def _iota(shape, dim):
    return lax.broadcasted_iota(jnp.int32, shape, dim)


def _split3(x):
    hi = x.astype(BF16)
    r1 = x - hi.astype(F32)
    mid = r1.astype(BF16)
    lo = (r1 - mid.astype(F32)).astype(BF16)
    return hi, mid, lo


def _dot_exact_rhs01(x, m01):
    hi, mid, lo = _split3(x)
    d = lambda t: jnp.dot(t, m01, preferred_element_type=F32)
    return d(hi) + d(mid) + d(lo)


def _log_sigmoid(z):
    return jnp.minimum(z, 0.0) - jnp.log1p(jnp.exp(-jnp.abs(z)))


def _silu(x):
    return x * jax.nn.sigmoid(x)


def _layer_norm(x, g, b):
    mu = jnp.mean(x, axis=-1, keepdims=True)
    xc = x - mu
    var = jnp.mean(xc * xc, axis=-1, keepdims=True)
    return xc * lax.rsqrt(var + EPS) * g + b


def _rms_norm(x, g):
    return x * lax.rsqrt(jnp.mean(x * x, axis=-1, keepdims=True) + EPS) * g


def _dot_nt(a, b):
    return lax.dot_general(a, b, (((1,), (1,)), ((), ())), preferred_element_type=F32)


def _dot_tn(a, b):
    return lax.dot_general(a, b, (((0,), (0,)), ((), ())), preferred_element_type=F32)


def _ada_kernel(c_ref, w_ref, b_ref, o_ref):
    a = _silu(c_ref[...]).astype(BF16)
    o_ref[0] = jnp.dot(a, w_ref[0].astype(BF16), preferred_element_type=F32) + b_ref[0]


def _ada_call(c_all, ada_w, ada_b):
    depth, d, e = ada_w.shape
    rows = c_all.shape[0]
    bn = 1024
    return pl.pallas_call(
        _ada_kernel,
        grid=(depth, e // bn),
        in_specs=[pl.BlockSpec((rows, d), lambda l, n: (0, 0)),
                  pl.BlockSpec((1, d, bn), lambda l, n: (l, 0, n)),
                  pl.BlockSpec((1, 1, bn), lambda l, n: (l, 0, n))],
        out_specs=pl.BlockSpec((1, rows, bn), lambda l, n: (l, 0, n)),
        out_shape=jax.ShapeDtypeStruct((depth, rows, e), F32),
        compiler_params=_cparams(("arbitrary", "arbitrary")),
        name="ada_mod",
    )(c_all, ada_w, ada_b.reshape(depth, 1, e))


def _proj_kernel(x_ref, mod_ref, tab_ref, w_in_ref, w_gate_ref, b_gate_ref, qn_g_ref, kvn_g_ref,
                 w_uq_ref, w_kv_ref,
                 g1_ref, gv_ref, gg_ref, r1_ref, rv_ref, rg_ref, mq_ref, mkv_ref):
    d = D_MODEL
    x = x_ref[0]
    mod = mod_ref[0]
    h = (x * (1.0 + mod[:, d:2 * d]) + mod[:, 0:d]).astype(BF16)
    p = jnp.dot(h, w_in_ref[...], preferred_element_type=F32)

    ret_cos = tab_ref[:, 0:128]
    ret_sin = tab_ref[:, 128:256]
    q_cos = tab_ref[:, 256:384]
    q_sin = tab_ref[:, 384:512]
    misc_tab = tab_ref[:, 512:640]

    misc = p[:, P_MISC:P_MISC + 128]
    z = jnp.dot(misc.astype(BF16), w_gate_ref[...], preferred_element_type=F32) + b_gate_ref[...]
    log_a = _log_sigmoid(z) / GLA_TAU
    g1_ref[0, :, 0:128] = p[:, P_GQ:P_GQ + 128] * (DK ** -0.5)
    g1_ref[0, :, 128:256] = p[:, P_GK:P_GK + 128]
    g1_ref[0, :, 256:512] = log_a
    gv_ref[0] = p[:, P_GV:P_GV + V_W].astype(BF16)
    gg_ref[0] = p[:, P_GG:P_GG + V_W]

    r1_ref[0, :, 0:128] = p[:, P_RQ:P_RQ + 128] * ret_cos + p[:, P_RQS:P_RQS + 128] * ret_sin
    ks = DK ** -0.5
    r1_ref[0, :, 128:256] = ((p[:, P_RK:P_RK + 128] * ks) * ret_cos
                             + (p[:, P_RKS:P_RKS + 128] * ks) * ret_sin)
    rv_ref[0] = p[:, P_RV:P_RV + V_W].astype(BF16)
    rg_ref[0] = p[:, P_RG:P_RG + V_W]

    cq = _rms_norm(p[:, P_CQ:P_CQ + MLA_Q_RANK], qn_g_ref[...]).astype(BF16)
    qq = jnp.dot(cq, w_uq_ref[...], preferred_element_type=F32)
    qw = MLA_HEADS * MLA_HEAD_PAD
    for hh in range(MLA_HEADS):
        lo = hh * MLA_HEAD_PAD
        mq_ref[0, :, lo:lo + MLA_HEAD_PAD] = (
            qq[:, lo:lo + MLA_HEAD_PAD] * q_cos + qq[:, qw + lo:qw + lo + MLA_HEAD_PAD] * q_sin
        ).astype(BF16)

    ckv = _rms_norm(p[:, P_CKV:P_CKV + MLA_KV_RANK], kvn_g_ref[...])
    prod = misc * misc_tab
    kr = prod + pltpu.roll(prod, 128 - MLA_D_ROPE, axis=1)
    lhs = jnp.concatenate([ckv, kr], axis=1).astype(BF16)
    mkv_ref[0] = jnp.dot(lhs, w_kv_ref[...], preferred_element_type=F32).astype(BF16)


def _proj_call(xa, mod_l, tab, wl, n_ctx_tiles):
    bsz, t_all, d = xa.shape
    nt = t_all // TM
    tile = lambda w: pl.BlockSpec((1, TM, w), lambda b, t: (b, t, 0))
    full = lambda a: pl.BlockSpec(a.shape, lambda b, t: (0,) * a.ndim)
    sds = lambda w, dt: jax.ShapeDtypeStruct((bsz, t_all, w), dt)
    weights = (wl["w_in"], wl["w_gate"], wl["b_gate"], wl["qn_g"], wl["kvn_g"], wl["w_uq"], wl["w_kv"])
    return pl.pallas_call(
        _proj_kernel,
        grid=(bsz, nt),
        in_specs=[tile(d),
                  pl.BlockSpec((1, 1, 6 * d), lambda b, t: (2 * b + (t >= n_ctx_tiles).astype(jnp.int32), 0, 0)),
                  pl.BlockSpec((TM, tab.shape[1]), lambda b, t: (t, 0))]
                 + [full(w) for w in weights],
        out_specs=[tile(512), tile(V_W), tile(V_W), tile(256), tile(V_W), tile(V_W),
                   tile(MLA_HEADS * MLA_HEAD_PAD), tile(2048)],
        out_shape=[sds(512, F32), sds(V_W, BF16), sds(V_W, F32), sds(256, F32), sds(V_W, BF16),
                   sds(V_W, F32), sds(MLA_HEADS * MLA_HEAD_PAD, BF16), sds(2048, BF16)],
        compiler_params=_cparams(("arbitrary", "arbitrary")),
        name="proj",
    )(xa, mod_l, tab, *weights)


def _key_lane_masks():
    lane_head = _iota((1, QK_W), 1) // DK
    return [jnp.where(lane_head == h, 1.0, 0.0).astype(BF16) for h in range(N_HEADS)]


def _state_mask():
    return jnp.where((_iota((V_W, QK_W), 0) // DV) == (_iota((V_W, QK_W), 1) // DK), 1.0, 0.0)


def _fill_head_values(v_ref, v4_ref, n_rows):
    lane_head = _iota((TM, V_W), 1) // DV

    def body(i, carry):
        r0 = pl.multiple_of(i * TM, TM)
        v = v_ref[0, pl.ds(r0, TM), :].astype(F32)
        for h in range(N_HEADS):
            v4_ref[h, pl.ds(r0, TM), :] = jnp.where(lane_head == h, v, 0.0).astype(BF16)
        return carry
    lax.fori_loop(0, n_rows // TM, body, 0)


def _chunk_step(qe, ke, weights, qdec, kdec, sdec, s, v_ref, v4_ref, r0, key_masks, state_mask):
    c = SCAN_CHUNK
    o = None
    for h in range(N_HEADS):
        att = _dot_nt(qe * key_masks[h], ke) * weights[h]
        part = jnp.dot(att.astype(BF16), v4_ref[h, pl.ds(r0, c), :], preferred_element_type=F32)
        o = part if o is None else o + part
    o = o + _dot_nt(qdec, s.astype(BF16))
    upd = _dot_tn(v_ref[0, pl.ds(r0, c), :], kdec)
    return o, s * sdec + upd * state_mask


def _run_scan(chunk_fn, sf_ref, sb_ref, of_ref, ob_ref, n_ctx_chunks, n_chunks):
    sf_ref[...] = jnp.zeros_like(sf_ref)
    sb_ref[...] = jnp.zeros_like(sb_ref)

    def body(i, carry):
        rf = pl.multiple_of(i * SCAN_CHUNK, SCAN_CHUNK)
        cb = jnp.where(i < n_ctx_chunks, n_ctx_chunks - 1 - i, n_chunks + n_ctx_chunks - 1 - i)
        rb = pl.multiple_of(cb * SCAN_CHUNK, SCAN_CHUNK)
        o_f, s_f = chunk_fn(rf, True, sf_ref[...])
        o_b, s_b = chunk_fn(rb, False, sb_ref[...])
        of_ref[pl.ds(rf, SCAN_CHUNK), :] = o_f
        ob_ref[pl.ds(rb, SCAN_CHUNK), :] = o_b
        sf_ref[...] = s_f
        sb_ref[...] = s_b
        return carry
    lax.fori_loop(0, n_chunks, body, 0)


def _group_ones():
    same = (_iota((V_W, V_W), 0) // DV) == (_iota((V_W, V_W), 1) // DV)
    return jnp.where(same, 1.0, 0.0).astype(BF16)


def _gla_kernel(g1_ref, gv_ref, gg_ref, ng_ref, out_ref, of_ref, ob_ref, sf_ref, sb_ref, v4_ref, *,
                n_ctx_chunks, n_chunks):
    c = SCAN_CHUNK
    key_masks = _key_lane_masks()
    state_mask = _state_mask()
    ti, si = _iota((c, c), 0), _iota((c, c), 1)
    causal = {True: jnp.where(si <= ti, 1.0, 0.0), False: jnp.where(si >= ti, 1.0, 0.0)}
    tri = {fwd: m.astype(BF16) for fwd, m in causal.items()}
    _fill_head_values(gv_ref, v4_ref, n_chunks * c)

    def chunk_fn(r0, fwd, s):
        q = g1_ref[0, pl.ds(r0, c), 0:128]
        k = g1_ref[0, pl.ds(r0, c), 128:256]
        a = g1_ref[0, pl.ds(r0, c), 256:384] if fwd else g1_ref[0, pl.ds(r0, c), 384:512]
        a_hi = a.astype(BF16)
        a_lo = (a - a_hi.astype(F32)).astype(BF16)
        b = (jnp.dot(tri[fwd], a_hi, preferred_element_type=F32)
             + jnp.dot(tri[fwd], a_lo, preferred_element_type=F32))
        b_mid = b[c // 2:c // 2 + 1, :]
        b_end = b[c - 1:c, :] if fwd else b[0:1, :]
        qe = (q * jnp.exp(b - b_mid)).astype(BF16)
        ke = (k * jnp.exp(b_mid - b)).astype(BF16)
        qdec = (q * jnp.exp(b)).astype(BF16)
        kdec = (k * jnp.exp(b_end - b)).astype(BF16)
        return _chunk_step(qe, ke, [causal[fwd]] * N_HEADS, qdec, kdec, jnp.exp(b_end), s,
                           gv_ref, v4_ref, r0, key_masks, state_mask)

    def token_scan():
        of_ref[...] = jnp.zeros_like(of_ref)
        ob_ref[...] = jnp.zeros_like(ob_ref)
        sf_ref[...] = jnp.zeros_like(sf_ref)
        sb_ref[...] = jnp.zeros_like(sb_ref)
        row = _iota((c, 1), 0)

        def token(r0, t, fwd):
            s_ref, o_ref = (sf_ref, of_ref) if fwd else (sb_ref, ob_ref)
            pick = row == t
            q = jnp.where(pick, g1_ref[0, pl.ds(r0, c), 0:128], 0.0).astype(BF16)
            k = jnp.where(pick, g1_ref[0, pl.ds(r0, c), 128:256], 0.0).astype(BF16)
            a = g1_ref[0, pl.ds(r0, c), 256:384] if fwd else g1_ref[0, pl.ds(r0, c), 384:512]
            a_t = jnp.sum(jnp.where(pick, a, 0.0), axis=0, keepdims=True)
            s = s_ref[...] * jnp.exp(a_t) + _dot_tn(gv_ref[0, pl.ds(r0, c), :], k) * state_mask
            s_ref[...] = s
            o_ref[pl.ds(r0, c), :] += _dot_nt(q, s.astype(BF16))

        def chunk_body(i, carry):
            rf = pl.multiple_of(i * c, c)
            cb = jnp.where(i < n_ctx_chunks, n_ctx_chunks - 1 - i, n_chunks + n_ctx_chunks - 1 - i)
            rb = pl.multiple_of(cb * c, c)

            def token_body(j, carry2):
                token(rf, j, True)
                token(rb, c - 1 - j, False)
                return carry2
            return lax.fori_loop(0, c, token_body, carry)
        lax.fori_loop(0, n_chunks, chunk_body, 0)

    def chunk_total(i, m):
        r0 = pl.multiple_of(i * c, c)
        return jnp.maximum(m, -jnp.sum(g1_ref[0, pl.ds(r0, c), 256:512], axis=0, keepdims=True))
    worst = jnp.max(lax.fori_loop(0, n_chunks, chunk_total, jnp.zeros((1, 2 * QK_W), F32)))
    chunked_ok = worst < GLA_MAX_CHUNK_LOG_DECAY

    @pl.when(chunked_ok)
    def _():
        _run_scan(chunk_fn, sf_ref, sb_ref, of_ref, ob_ref, n_ctx_chunks, n_chunks)

    @pl.when(jnp.logical_not(chunked_ok))
    def _():
        token_scan()

    ones_g = _group_ones()
    ng = ng_ref[...]

    for i in range((n_chunks * c) // TM):
        rows = slice(i * TM, (i + 1) * TM)
        o = of_ref[rows, :] + ob_ref[rows, :]
        ms = _dot_exact_rhs01(o * o, ones_g) * (1.0 / DV)
        y = o * lax.rsqrt(ms + EPS) * ng * _silu(gg_ref[0, rows, :])
        out_ref[0, rows, :] = y.astype(BF16)


def _ret_kernel(r1_ref, rv_ref, rg_ref, dec_head_ref, dec_lanes_ref, out_ref, of_ref, ob_ref, sf_ref,
                sb_ref, v4_ref, dm_ref, *, n_ctx_chunks, n_chunks):
    c = SCAN_CHUNK
    key_masks = _key_lane_masks()
    state_mask = _state_mask()
    ti, si = _iota((c, c), 0), _iota((c, c), 1)
    tl = _iota((c, QK_W), 0).astype(F32)
    _fill_head_values(rv_ref, v4_ref, n_chunks * c)

    consts = []
    for di in range(2):
        fwd = di == 0
        rel = (ti - si) if fwd else (si - ti)
        relf = jnp.maximum(rel, 0).astype(F32)
        for h in range(N_HEADS):
            lg = _log_sigmoid(dec_head_ref[di, h])
            dm_ref[di, h] = jnp.where(rel >= 0, jnp.exp(relf * lg), 0.0)
        lg_lanes = _log_sigmoid(dec_lanes_ref[di])
        n = tl if fwd else (c - 1.0) - tl
        consts.append((jnp.exp((n + 1.0) * lg_lanes), jnp.exp((c - 1.0 - n) * lg_lanes),
                       jnp.exp(float(c) * lg_lanes)))

    def chunk_fn(r0, fwd, s):
        di = 0 if fwd else 1
        q_dec, k_dec, s_dec = consts[di]
        q = r1_ref[0, pl.ds(r0, c), 0:128]
        k = r1_ref[0, pl.ds(r0, c), 128:256]
        return _chunk_step(q.astype(BF16), k.astype(BF16), [dm_ref[di, h] for h in range(N_HEADS)],
                           (q * q_dec).astype(BF16), (k * k_dec).astype(BF16), s_dec, s,
                           rv_ref, v4_ref, r0, key_masks, state_mask)

    _run_scan(chunk_fn, sf_ref, sb_ref, of_ref, ob_ref, n_ctx_chunks, n_chunks)

    ones_g = _group_ones()

    for i in range((n_chunks * c) // TM):
        rows = slice(i * TM, (i + 1) * TM)
        o = of_ref[rows, :] + ob_ref[rows, :]
        mu = _dot_exact_rhs01(o, ones_g) * (1.0 / DV)
        oc = o - mu
        var = _dot_exact_rhs01(oc * oc, ones_g) * (1.0 / DV)
        y = oc * lax.rsqrt(var + EPS) * _silu(rg_ref[0, rows, :])
        out_ref[0, rows, :] = y.astype(BF16)


def _scan_call(kernel, name, seq_inputs, small_inputs, extra_scratch, n_ctx):
    bsz, t_all, _ = seq_inputs[0].shape
    assert t_all % SCAN_CHUNK == 0 and n_ctx % SCAN_CHUNK == 0
    per_b = lambda a: pl.BlockSpec((1,) + a.shape[1:], lambda b: (b, 0, 0))
    full = lambda a: pl.BlockSpec(a.shape, lambda b: (0,) * a.ndim)
    state = pltpu.VMEM((V_W, QK_W), F32)
    o_dir = pltpu.VMEM((t_all, V_W), F32)
    return pl.pallas_call(
        functools.partial(kernel, n_ctx_chunks=n_ctx // SCAN_CHUNK, n_chunks=t_all // SCAN_CHUNK),
        grid=(bsz,),
        in_specs=[per_b(a) for a in seq_inputs] + [full(a) for a in small_inputs],
        out_specs=pl.BlockSpec((1, t_all, V_W), lambda b: (b, 0, 0)),
        out_shape=jax.ShapeDtypeStruct((bsz, t_all, V_W), BF16),
        scratch_shapes=[o_dir, o_dir, state, state, pltpu.VMEM((N_HEADS, t_all, V_W), BF16)] + extra_scratch,
        compiler_params=_cparams(("arbitrary",)),
        name=name,
    )(*seq_inputs, *small_inputs)


def _mla_kernel(q_ref, kv_ref, x_ref, mg_ref, mr_ref, mod_ref, w_ref, g_ref, b_ref, o_ref, mm_ref, *,
                n_ctx, t_off):
    d = D_MODEL
    t = pl.program_id(1) + t_off
    kw = MLA_HEADS * MLA_HEAD_PAD
    vw = MLA_HEADS * MLA_DV
    first_half = _iota((1, 128), 1) < MLA_DV

    def attend(n_keys):
        for pair in range(MLA_HEADS // 2):
            pv, rinv = [], []
            for hh in range(2):
                lo = (2 * pair + hh) * MLA_HEAD_PAD
                s = _dot_nt(q_ref[0, :, lo:lo + MLA_HEAD_PAD], kv_ref[0, 0:n_keys, lo:lo + MLA_HEAD_PAD])
                e = jnp.exp2(s - jnp.max(s, axis=-1, keepdims=True))
                rinv.append(1.0 / jnp.sum(e, axis=-1, keepdims=True))
                vlo = kw + hh * vw + pair * 128
                pv.append(jnp.dot(e.astype(BF16), kv_ref[0, 0:n_keys, vlo:vlo + 128],
                                  preferred_element_type=F32))
            scale = jnp.where(first_half, rinv[0], rinv[1])
            mm_ref[:, pair * 128:(pair + 1) * 128] = ((pv[0] + pv[1]) * scale).astype(BF16)

    def project():
        w = V_W
        y = (jnp.dot(mg_ref[0], w_ref[0:w, :], preferred_element_type=F32)
             + jnp.dot(mr_ref[0], w_ref[w:2 * w, :], preferred_element_type=F32)
             + jnp.dot(mm_ref[...], w_ref[2 * w:, :], preferred_element_type=F32))
        g1 = mod_ref[0][:, 2 * d:3 * d]
        o_ref[0] = _layer_norm(ALPHA * x_ref[0] + g1 * y, g_ref[...], b_ref[...])

    n_ctx_tiles = n_ctx // TM

    @pl.when(t < n_ctx_tiles)
    def _():
        attend(n_ctx)
        project()

    @pl.when(t >= n_ctx_tiles)
    def _():
        attend(kv_ref.shape[1])
        project()


def _mla_call(mq, mkv, xa, m_gla, m_ret, mod_l, wl, n_ctx, t_off):
    bsz, t_all, d = xa.shape
    nt = t_all // TM - t_off
    n_ctx_tiles = n_ctx // TM
    src = lambda w: pl.BlockSpec((1, TM, w), lambda b, t: (b, t + t_off, 0))
    full = lambda a: pl.BlockSpec(a.shape, lambda b, t: (0,) * a.ndim)
    return pl.pallas_call(
        functools.partial(_mla_kernel, n_ctx=n_ctx, t_off=t_off),
        grid=(bsz, nt),
        in_specs=[src(mq.shape[2]),
                  pl.BlockSpec((1, t_all, mkv.shape[2]), lambda b, t: (b, 0, 0)),
                  src(d), src(V_W), src(V_W),
                  pl.BlockSpec((1, 1, 6 * d),
                               lambda b, t: (2 * b + (t + t_off >= n_ctx_tiles).astype(jnp.int32), 0, 0)),
                  full(wl["w_out"]), full(wl["ln1_g"]), full(wl["ln1_b"])],
        out_specs=pl.BlockSpec((1, TM, d), lambda b, t: (b, t, 0)),
        out_shape=jax.ShapeDtypeStruct((bsz, nt * TM, d), F32),
        scratch_shapes=[pltpu.VMEM((TM, MLA_HEADS * MLA_DV), BF16)],
        compiler_params=_cparams(("arbitrary", "arbitrary")),
        name="mla_attn_out_ln1",
    )(mq, mkv, xa, m_gla, m_ret, mod_l, wl["w_out"], wl["ln1_g"], wl["ln1_b"])


def _ffn_kernel(xp_ref, x_ref, xn_ref, mod_ref, up_ref, cw_ref, cb_ref, down_ref, g_ref, b_ref, o_ref,
                h2_ref, ua_ref, ug_ref, act_ref, *, seq_starts, seq_ends):
    d = D_MODEL
    n_rows = TM + 2 * HALO
    n_ff = D_FF // FF_CHUNK
    t = pl.program_id(1)
    mod = mod_ref[0]
    sh2, sc2, g2 = mod[:, 3 * d:4 * d], mod[:, 4 * d:5 * d], mod[:, 5 * d:6 * d]
    x1 = x_ref[0]
    h2_ref[0:HALO, :] = (xp_ref[0] * (1.0 + sc2) + sh2).astype(BF16)
    h2_ref[HALO:HALO + TM, :] = (x1 * (1.0 + sc2) + sh2).astype(BF16)
    h2_ref[HALO + TM:n_rows, :] = (xn_ref[0] * (1.0 + sc2) + sh2).astype(BF16)

    is_start = functools.reduce(jnp.logical_or, [t == s for s in seq_starts])
    is_end = functools.reduce(jnp.logical_or, [t == e for e in seq_ends])
    keep_prev = jnp.where(is_start, 0.0, 1.0)
    keep_next = jnp.where(is_end, 0.0, 1.0)

    def up_stage(j):
        slot = j % 2
        for u_ref, lo in ((ua_ref, j * FF_CHUNK), (ug_ref, D_FF + j * FF_CHUNK)):
            u = jnp.dot(h2_ref[...], up_ref[:, lo:lo + FF_CHUNK], preferred_element_type=F32)
            u_ref[slot, 0:HALO, :] = u[0:HALO] * keep_prev
            u_ref[slot, HALO:HALO + TM, :] = u[HALO:HALO + TM]
            u_ref[slot, HALO + TM:n_rows, :] = u[HALO + TM:n_rows] * keep_next

    def conv(u_ref, slot, lo):
        cw = cw_ref[:, lo:lo + FF_CHUNK]
        return (u_ref[slot, HALO - 1:HALO - 1 + TM, :] * cw[0:1]
                + u_ref[slot, HALO:HALO + TM, :] * cw[1:2]
                + u_ref[slot, HALO + 1:HALO + 1 + TM, :] * cw[2:3]
                + cb_ref[:, lo:lo + FF_CHUNK])

    def act_stage(j):
        slot = j % 2
        a = conv(ua_ref, slot, j * FF_CHUNK)
        gt = conv(ug_ref, slot, D_FF + j * FF_CHUNK)
        act_ref[:, j * FF_CHUNK:(j + 1) * FF_CHUNK] = (_silu(a) * gt).astype(BF16)

    half = (n_ff + 1) // 2 * FF_CHUNK
    down = lambda lo, hi: jnp.dot(act_ref[:, lo:hi], down_ref[lo:hi, :], preferred_element_type=F32)
    acc = None
    up_stage(0)
    for j in range(n_ff):
        if j + 1 < n_ff:
            up_stage(j + 1)
        act_stage(j)
        if (j + 1) * FF_CHUNK == half:
            acc = down(0, half)
    acc = acc + down(half, D_FF)
    o_ref[0] = _layer_norm(ALPHA * x1 + g2 * acc, g_ref[...], b_ref[...])


def _ffn_call(x1, mod_l, wl, n_ctx_tiles, t_off):
    bsz, rows, d = x1.shape
    nt = rows // TM
    hb = TM // HALO
    n_hblk = rows // HALO
    starts = sorted({max(0 - t_off, 0), max(n_ctx_tiles - t_off, 0)})
    ends = sorted({e for e in (n_ctx_tiles - 1 - t_off, nt - 1) if e >= 0})
    full = lambda a: pl.BlockSpec(a.shape, lambda b, t: (0,) * a.ndim)
    weights = (wl["ffn_up"], wl["conv_w"], wl["conv_b"], wl["ffn_down"], wl["ln2_g"], wl["ln2_b"])
    n_rows = TM + 2 * HALO
    return pl.pallas_call(
        functools.partial(_ffn_kernel, seq_starts=tuple(starts), seq_ends=tuple(ends)),
        grid=(bsz, nt),
        in_specs=[pl.BlockSpec((1, HALO, d), lambda b, t: (b, jnp.maximum(t * hb - 1, 0), 0)),
                  pl.BlockSpec((1, TM, d), lambda b, t: (b, t, 0)),
                  pl.BlockSpec((1, HALO, d), lambda b, t: (b, jnp.minimum((t + 1) * hb, n_hblk - 1), 0)),
                  pl.BlockSpec((1, 1, 6 * d),
                               lambda b, t: (2 * b + (t + t_off >= n_ctx_tiles).astype(jnp.int32), 0, 0))]
                 + [full(w) for w in weights],
        out_specs=pl.BlockSpec((1, TM, d), lambda b, t: (b, t, 0)),
        out_shape=jax.ShapeDtypeStruct((bsz, rows, d), F32),
        scratch_shapes=[pltpu.VMEM((n_rows, d), BF16),
                        pltpu.VMEM((2, n_rows, FF_CHUNK), F32), pltpu.VMEM((2, n_rows, FF_CHUNK), F32),
                        pltpu.VMEM((TM, D_FF), BF16)],
        compiler_params=_cparams(("arbitrary", "arbitrary")),
        name="conv_ffn_ln2",
    )(x1, x1, x1, mod_l, *weights)


def _rope_tables(seq, n_ctx):
    pos = jnp.arange(seq, dtype=F32)
    ret_inv = 1.0 / (ROPE_BASE ** jnp.linspace(0.0, 1.0, DK // 2, dtype=F32))
    ret_ang = pos[:, None] * ret_inv
    rc, rs = jnp.cos(ret_ang), jnp.sin(ret_ang)
    ret_cos = jnp.tile(jnp.concatenate([rc, rc], axis=1), (1, N_HEADS))
    ret_sin = jnp.tile(jnp.concatenate([-rs, rs], axis=1), (1, N_HEADS))

    rows = jnp.repeat(jnp.arange(seq // GRID_W, dtype=F32), GRID_W)
    cols = jnp.tile(jnp.arange(GRID_W, dtype=F32), seq // GRID_W)
    n_ax = MLA_D_ROPE // 4
    ax_inv = ROPE_BASE ** (-jnp.arange(n_ax, dtype=F32) / n_ax)
    ra, ca = rows[:, None] * ax_inv, cols[:, None] * ax_inv
    cos32 = jnp.concatenate([jnp.cos(ra), jnp.cos(ra), jnp.cos(ca), jnp.cos(ca)], axis=1)
    sin32 = jnp.concatenate([-jnp.sin(ra), jnp.sin(ra), -jnp.sin(ca), jnp.sin(ca)], axis=1)
    one, zero = jnp.ones((seq, MLA_D_NOPE), F32), jnp.zeros((seq, 32), F32)
    q_cos = MLA_QSCALE * jnp.concatenate([one, cos32, zero], axis=1)
    q_sin = MLA_QSCALE * jnp.concatenate([0.0 * one, sin32, zero], axis=1)
    misc = jnp.concatenate([cos32, sin32, zero, zero], axis=1)
    lat = jnp.concatenate([ret_cos, ret_sin, q_cos, q_sin, misc], axis=1)

    c1, c0 = jnp.ones((n_ctx, 128), F32), jnp.zeros((n_ctx, 128), F32)
    q_cos_c = MLA_QSCALE * jnp.concatenate([jnp.ones((n_ctx, 96), F32), jnp.zeros((n_ctx, 32), F32)], axis=1)
    misc_c = jnp.concatenate([jnp.ones((n_ctx, 32), F32), jnp.zeros((n_ctx, 96), F32)], axis=1)
    ctx = jnp.concatenate([c1, c0, q_cos_c, c0, misc_c], axis=1)
    return jnp.concatenate([ctx, lat], axis=0)


def _swap_idx(n, half):
    j = np.arange(n)
    return (j // (2 * half)) * (2 * half) + (j % (2 * half) + half) % (2 * half)


def _layer_weights(i, w_in, gla_gate_w, gla_gate_b, gla_norm_g, ret_decay, mla_q_norm_g, mla_kv_norm_g,
                   mla_w_uq, mla_w_uk, mla_w_uv, w_out, ln1_g, ln1_b, ffn_up, ffn_conv_w, ffn_conv_b,
                   ffn_down, ln2_g, ln2_b):
    d = D_MODEL
    o = IN_OFFS
    wi = w_in[i]
    sec = lambda k: wi[:, o[k]:o[k + 1]]
    swap_ret = _swap_idx(QK_W, DK // 2)
    swap_kr = _swap_idx(MLA_D_ROPE, MLA_D_ROPE // 4)
    kr = sec(11)
    misc = jnp.concatenate([kr, kr[:, swap_kr], sec(3), jnp.zeros((d, 32), F32)], axis=1)
    w_in_p = jnp.concatenate([sec(0), sec(1), sec(2), sec(4), sec(5), sec(6), sec(7), sec(8), sec(9),
                              sec(10), misc, sec(5)[:, swap_ret], sec(6)[:, swap_ret]], axis=1).astype(BF16)

    wg = jnp.zeros((128, 2 * QK_W), F32)
    wg = wg.at[64:64 + GATE_RANK, 0:QK_W].set(gla_gate_w[i, 0])
    wg = wg.at[64 + GATE_RANK:64 + 2 * GATE_RANK, QK_W:].set(gla_gate_w[i, 1])
    bg = jnp.concatenate([gla_gate_b[i, 0], gla_gate_b[i, 1]])[None, :]

    uq = mla_w_uq[i].reshape(MLA_Q_RANK, MLA_HEADS, MLA_D_NOPE + MLA_D_ROPE)
    z32 = jnp.zeros((MLA_Q_RANK, MLA_HEADS, 32), F32)
    z64 = jnp.zeros((MLA_Q_RANK, MLA_HEADS, 64), F32)
    uq_main = jnp.concatenate([uq, z32], axis=2).reshape(MLA_Q_RANK, -1)
    uq_sw = jnp.concatenate([z64, uq[:, :, MLA_D_NOPE:][:, :, swap_kr], z32], axis=2).reshape(MLA_Q_RANK, -1)
    w_uq = jnp.concatenate([uq_main, uq_sw], axis=1).astype(BF16)

    uk = mla_w_uk[i].reshape(MLA_KV_RANK, MLA_HEADS, MLA_D_NOPE)
    uk_pad = jnp.concatenate([uk, jnp.zeros((MLA_KV_RANK, MLA_HEADS, 64), F32)], axis=2).reshape(MLA_KV_RANK, -1)
    uv = mla_w_uv[i].reshape(MLA_KV_RANK, MLA_HEADS, MLA_DV)
    even = (jnp.arange(MLA_HEADS) % 2 == 0)[None, :, None]
    uv_a = jnp.where(even, uv, 0.0).reshape(MLA_KV_RANK, -1)
    uv_b = jnp.where(even, 0.0, uv).reshape(MLA_KV_RANK, -1)
    place = jnp.zeros((128, MLA_HEADS, MLA_HEAD_PAD), F32)
    eye = jnp.eye(MLA_D_ROPE, dtype=F32)
    place = place.at[0:MLA_D_ROPE, :, MLA_D_NOPE:MLA_D_NOPE + MLA_D_ROPE].set(
        jnp.broadcast_to(eye[:, None, :], (MLA_D_ROPE, MLA_HEADS, MLA_D_ROPE)))
    place = place.reshape(128, -1)
    w_kv = jnp.concatenate([
        jnp.concatenate([uk_pad, uv_a, uv_b], axis=1),
        jnp.concatenate([place, jnp.zeros((128, 2 * MLA_HEADS * MLA_DV), F32)], axis=1)], axis=0).astype(BF16)

    row = lambda a: a[None, :]
    return {
        "w_in": w_in_p, "w_gate": wg.astype(BF16), "b_gate": bg,
        "qn_g": row(mla_q_norm_g[i]), "kvn_g": row(mla_kv_norm_g[i]), "w_uq": w_uq, "w_kv": w_kv,
        "gla_ng": row(jnp.tile(gla_norm_g[i], N_HEADS)),
        "dec_head": jnp.broadcast_to(ret_decay[i][:, :, None, None], (2, N_HEADS, 1, SCAN_CHUNK)),
        "dec_lanes": jnp.repeat(ret_decay[i], DK, axis=1)[:, None, :],
        "w_out": w_out[i].astype(BF16), "ln1_g": row(ln1_g[i]), "ln1_b": row(ln1_b[i]),
        "ffn_up": ffn_up[i].astype(BF16), "conv_w": ffn_conv_w[i], "conv_b": row(ffn_conv_b[i]),
        "ffn_down": ffn_down[i].astype(BF16), "ln2_g": row(ln2_g[i]), "ln2_b": row(ln2_b[i]),
    }


def kernel(x, c, ctx, c_ctx, ada_w, ada_b, w_in, gla_gate_w, gla_gate_b, gla_norm_g, ret_decay, mla_q_norm_g, mla_kv_norm_g, mla_w_uq, mla_w_uk, mla_w_uv, w_out, ln1_g, ln1_b, ffn_up, ffn_conv_w, ffn_conv_b, ffn_down, ln2_g, ln2_b):
    bsz, seq, d = x.shape
    n_ctx = ctx.shape[1]
    assert d == D_MODEL and seq % TM == 0 and n_ctx % TM == 0 and seq % GRID_W == 0
    n_ctx_tiles = n_ctx // TM

    rows = -(-(bsz + 1) // 8) * 8
    c_all = jnp.concatenate([c, c_ctx[None, :], jnp.zeros((rows - bsz - 1, d), F32)], axis=0)
    mod = _ada_call(c_all, ada_w, ada_b)
    mod_ctx = jnp.broadcast_to(mod[:, bsz:bsz + 1], (DEPTH, bsz, 6 * d))
    mod_sel = jnp.stack([mod_ctx, mod[:, :bsz]], axis=2).reshape(DEPTH, 2 * bsz, 1, 6 * d)

    tab = _rope_tables(seq, n_ctx)
    xa = jnp.concatenate([ctx, x], axis=1)
    dmat_scratch = [pltpu.VMEM((2, N_HEADS, SCAN_CHUNK, SCAN_CHUNK), F32)]

    for i in range(DEPTH):
        wl = _layer_weights(i, w_in, gla_gate_w, gla_gate_b, gla_norm_g, ret_decay, mla_q_norm_g,
                            mla_kv_norm_g, mla_w_uq, mla_w_uk, mla_w_uv, w_out, ln1_g, ln1_b, ffn_up,
                            ffn_conv_w, ffn_conv_b, ffn_down, ln2_g, ln2_b)
        mod_l = mod_sel[i]
        need_ctx = i < DEPTH - 1
        t_off = 0 if need_ctx else n_ctx_tiles
        g1, gv, gg, r1, rv, rg, mq, mkv = _proj_call(xa, mod_l, tab, wl, n_ctx_tiles)
        m_gla = _scan_call(_gla_kernel, "gla_scan", (g1, gv, gg), (wl["gla_ng"],), [], n_ctx)
        m_ret = _scan_call(_ret_kernel, "ret_scan", (r1, rv, rg), (wl["dec_head"], wl["dec_lanes"]),
                           dmat_scratch, n_ctx)
        x1 = _mla_call(mq, mkv, xa, m_gla, m_ret, mod_l, wl, n_ctx, t_off)
        xa = _ffn_call(x1, mod_l, wl, n_ctx_tiles, t_off)
    return xa
```

```python
import functools

import numpy as np
import jax
import jax.numpy as jnp
from jax import lax
from jax.experimental import pallas as pl
from jax.experimental.pallas import tpu as pltpu

F32 = jnp.float32
BF16 = jnp.bfloat16

D_MODEL = 1024
DEPTH = 4
GRID_W = 64
N_HEADS = 4
DK = 32
DV = 64
QK_W = N_HEADS * DK
V_W = N_HEADS * DV
GATE_RANK = 16
GLA_TAU = 16.0
MLA_HEADS = 8
MLA_D_NOPE = 64
MLA_D_ROPE = 32
MLA_DV = 64
MLA_Q_RANK = 256
MLA_KV_RANK = 128
MLA_HEAD_PAD = 128
MLA_SCALE = (MLA_D_NOPE + MLA_D_ROPE) ** -0.5
MLA_QSCALE = MLA_SCALE * float(np.log2(np.e))
D_FF = 2816
ROPE_BASE = 10000.0
EPS = 1e-6
ALPHA = (2 * DEPTH) ** 0.25
IN_SIZES = (QK_W, QK_W, V_W, 2 * GATE_RANK, V_W, QK_W, QK_W, V_W, V_W,
            MLA_Q_RANK, MLA_KV_RANK, MLA_D_ROPE)
IN_OFFS = tuple(int(o) for o in np.concatenate([[0], np.cumsum(IN_SIZES)]))

P_GQ, P_GK, P_GV, P_GG = 0, 128, 256, 512
P_RQ, P_RK, P_RV, P_RG = 768, 896, 1024, 1280
P_CQ, P_CKV, P_MISC = 1536, 1792, 1920
P_RQS, P_RKS = 2048, 2176
P_W = 2304

TM = 256
SCAN_CHUNK = 256
GLA_MAX_CHUNK_LOG_DECAY = 64.0
HALO = 8
FF_CHUNK = 256
FFN_TILES_PER_BLOCK = (3, 4, 2, 1)
DOWN_EVERY = 6
VMEM_LIMIT = 56 * 1024 * 1024


def _cparams(sem):
    return pltpu.CompilerParams(dimension_semantics=sem, vmem_limit_bytes=VMEM_LIMIT)


def _iota(shape, dim):
    return lax.broadcasted_iota(jnp.int32, shape, dim)


def _split3(x):
    hi = x.astype(BF16)
    r1 = x - hi.astype(F32)
    mid = r1.astype(BF16)
    lo = (r1 - mid.astype(F32)).astype(BF16)
    return hi, mid, lo


def _dot_exact_rhs01(x, m01):
    hi, mid, lo = _split3(x)
    d = lambda t: jnp.dot(t, m01, preferred_element_type=F32)
    return d(hi) + d(mid) + d(lo)


def _log_sigmoid(z):
    return jnp.minimum(z, 0.0) - jnp.log1p(jnp.exp(-jnp.abs(z)))


def _silu(x):
    return x * jax.nn.sigmoid(x)


def _layer_norm(x, g, b):
    mu = jnp.mean(x, axis=-1, keepdims=True)
    xc = x - mu
    var = jnp.mean(xc * xc, axis=-1, keepdims=True)
    return xc * lax.rsqrt(var + EPS) * g + b


def _rms_norm(x, g):
    return x * lax.rsqrt(jnp.mean(x * x, axis=-1, keepdims=True) + EPS) * g


def _dot_nt(a, b):
    return lax.dot_general(a, b, (((1,), (1,)), ((), ())), preferred_element_type=F32)


def _dot_tn(a, b):
    return lax.dot_general(a, b, (((0,), (0,)), ((), ())), preferred_element_type=F32)


def _ada_kernel(c_ref, w_ref, b_ref, o_ref):
    a = _silu(c_ref[...]).astype(BF16)
    o_ref[0] = jnp.dot(a, w_ref[0].astype(BF16), preferred_element_type=F32) + b_ref[0]


def _ada_call(c_all, ada_w, ada_b):
    depth, d, e = ada_w.shape
    rows = c_all.shape[0]
    bn = 1024
    return pl.pallas_call(
        _ada_kernel,
        grid=(depth, e // bn),
        in_specs=[pl.BlockSpec((rows, d), lambda l, n: (0, 0)),
                  pl.BlockSpec((1, d, bn), lambda l, n: (l, 0, n)),
                  pl.BlockSpec((1, 1, bn), lambda l, n: (l, 0, n))],
        out_specs=pl.BlockSpec((1, rows, bn), lambda l, n: (l, 0, n)),
        out_shape=jax.ShapeDtypeStruct((depth, rows, e), F32),
        compiler_params=_cparams(("arbitrary", "arbitrary")),
        name="ada_mod",
    )(c_all, ada_w, ada_b.reshape(depth, 1, e))


def _proj_kernel(x_ref, mod_ref, tab_ref, w_in_ref, w_gate_ref, b_gate_ref, qn_g_ref, kvn_g_ref,
                 w_uq_ref, w_kv_ref,
                 g1_ref, gv_ref, gg_ref, r1_ref, rv_ref, rg_ref, mq_ref, mk_ref, mvt_ref):
    d = D_MODEL
    x = x_ref[0]
    mod = mod_ref[0]
    h = (x * (1.0 + mod[:, d:2 * d]) + mod[:, 0:d]).astype(BF16)
    p = jnp.dot(h, w_in_ref[...], preferred_element_type=F32)

    ret_cos = tab_ref[:, 0:128]
    ret_sin = tab_ref[:, 128:256]
    q_cos = tab_ref[:, 256:384]
    q_sin = tab_ref[:, 384:512]
    misc_tab = tab_ref[:, 512:640]

    misc = p[:, P_MISC:P_MISC + 128]
    z = jnp.dot(misc.astype(BF16), w_gate_ref[...], preferred_element_type=F32) + b_gate_ref[...]
    log_a = _log_sigmoid(z) / GLA_TAU
    g1_ref[0, :, 0:128] = p[:, P_GQ:P_GQ + 128] * (DK ** -0.5)
    g1_ref[0, :, 128:256] = p[:, P_GK:P_GK + 128]
    g1_ref[0, :, 256:512] = log_a
    gv_ref[0] = p[:, P_GV:P_GV + V_W].astype(BF16)
    gg_ref[0] = p[:, P_GG:P_GG + V_W]

    r1_ref[0, :, 0:128] = p[:, P_RQ:P_RQ + 128] * ret_cos + p[:, P_RQS:P_RQS + 128] * ret_sin
    ks = DK ** -0.5
    r1_ref[0, :, 128:256] = ((p[:, P_RK:P_RK + 128] * ks) * ret_cos
                             + (p[:, P_RKS:P_RKS + 128] * ks) * ret_sin)
    rv_ref[0] = p[:, P_RV:P_RV + V_W].astype(BF16)
    rg_ref[0] = p[:, P_RG:P_RG + V_W]

    cq = _rms_norm(p[:, P_CQ:P_CQ + MLA_Q_RANK], qn_g_ref[...]).astype(BF16)
    qq = jnp.dot(cq, w_uq_ref[...], preferred_element_type=F32)
    qw = MLA_HEADS * MLA_HEAD_PAD
    for hh in range(MLA_HEADS):
        lo = hh * MLA_HEAD_PAD
        mq_ref[0, :, lo:lo + MLA_HEAD_PAD] = (
            qq[:, lo:lo + MLA_HEAD_PAD] * q_cos + qq[:, qw + lo:qw + lo + MLA_HEAD_PAD] * q_sin
        ).astype(BF16)

    ckv = _rms_norm(p[:, P_CKV:P_CKV + MLA_KV_RANK], kvn_g_ref[...])
    prod = misc * misc_tab
    kr = prod + pltpu.roll(prod, 128 - MLA_D_ROPE, axis=1)
    lhs = jnp.concatenate([ckv, kr], axis=1).astype(BF16)
    kv = jnp.dot(lhs, w_kv_ref[...], preferred_element_type=F32)
    kw = MLA_HEADS * MLA_HEAD_PAD
    mk_ref[0] = kv[:, 0:kw].astype(BF16)
    mvt_ref[0] = kv[:, kw:].T.astype(BF16)


def _proj_call(xa, mod_l, tab, wl, n_ctx_tiles):
    bsz, t_all, d = xa.shape
    nt = t_all // TM
    tile = lambda w: pl.BlockSpec((1, TM, w), lambda b, t: (b, t, 0))
    full = lambda a: pl.BlockSpec(a.shape, lambda b, t: (0,) * a.ndim)
    sds = lambda w, dt: jax.ShapeDtypeStruct((bsz, t_all, w), dt)
    weights = (wl["w_in"], wl["w_gate"], wl["b_gate"], wl["qn_g"], wl["kvn_g"], wl["w_uq"], wl["w_kv"])
    return pl.pallas_call(
        _proj_kernel,
        grid=(bsz, nt),
        in_specs=[tile(d),
                  pl.BlockSpec((1, 1, 6 * d), lambda b, t: (2 * b + (t >= n_ctx_tiles).astype(jnp.int32), 0, 0)),
                  pl.BlockSpec((TM, tab.shape[1]), lambda b, t: (t, 0))]
                 + [full(w) for w in weights],
        out_specs=[tile(512), tile(V_W), tile(V_W), tile(256), tile(V_W), tile(V_W),
                   tile(MLA_HEADS * MLA_HEAD_PAD), tile(MLA_HEADS * MLA_HEAD_PAD),
                   pl.BlockSpec((1, 2 * MLA_HEADS * MLA_DV, TM), lambda b, t: (b, 0, t))],
        out_shape=[sds(512, F32), sds(V_W, BF16), sds(V_W, F32), sds(256, F32), sds(V_W, BF16),
                   sds(V_W, F32), sds(MLA_HEADS * MLA_HEAD_PAD, BF16), sds(MLA_HEADS * MLA_HEAD_PAD, BF16),
                   jax.ShapeDtypeStruct((bsz, 2 * MLA_HEADS * MLA_DV, t_all), BF16)],
        compiler_params=_cparams(("arbitrary", "arbitrary")),
        name="proj",
    )(xa, mod_l, tab, *weights)


def _key_lane_masks():
    lane_head = _iota((1, QK_W), 1) // DK
    return [jnp.where(lane_head == h, 1.0, 0.0).astype(BF16) for h in range(N_HEADS)]


def _state_mask():
    return jnp.where((_iota((V_W, QK_W), 0) // DV) == (_iota((V_W, QK_W), 1) // DK), 1.0, 0.0)


def _fill_head_values(v_ref, v4_ref, n_rows):
    lane_head = _iota((TM, V_W), 1) // DV

    def body(i, carry):
        r0 = pl.multiple_of(i * TM, TM)
        v = v_ref[0, pl.ds(r0, TM), :].astype(F32)
        for h in range(N_HEADS):
            v4_ref[h, pl.ds(r0, TM), :] = jnp.where(lane_head == h, v, 0.0).astype(BF16)
        return carry
    lax.fori_loop(0, n_rows // TM, body, 0)


def _chunk_step(qe, ke, weights, qdec, kdec, sdec, s, v_ref, v4_ref, r0, key_masks, state_mask):
    c = SCAN_CHUNK
    o = None
    for h in range(N_HEADS):
        att = _dot_nt(qe * key_masks[h], ke) * weights[h]
        part = jnp.dot(att.astype(BF16), v4_ref[h, pl.ds(r0, c), :], preferred_element_type=F32)
        o = part if o is None else o + part
    o = o + _dot_nt(qdec, s.astype(BF16))
    upd = _dot_tn(v_ref[0, pl.ds(r0, c), :], kdec)
    return o, s * sdec + upd * state_mask


def _run_scan(chunk_fn, sf_ref, sb_ref, of_ref, ob_ref, n_ctx_chunks, n_chunks):
    s_f = jnp.zeros(sf_ref.shape, F32)
    s_b = jnp.zeros(sb_ref.shape, F32)
    for i in range(n_chunks):
        cb = n_ctx_chunks - 1 - i if i < n_ctx_chunks else n_chunks + n_ctx_chunks - 1 - i
        o_f, s_f = chunk_fn(i * SCAN_CHUNK, True, s_f)
        o_b, s_b = chunk_fn(cb * SCAN_CHUNK, False, s_b)
        of_ref[i * SCAN_CHUNK:(i + 1) * SCAN_CHUNK, :] = o_f
        ob_ref[cb * SCAN_CHUNK:(cb + 1) * SCAN_CHUNK, :] = o_b


def _group_ones():
    same = (_iota((V_W, V_W), 0) // DV) == (_iota((V_W, V_W), 1) // DV)
    return jnp.where(same, 1.0, 0.0).astype(BF16)


def _gla_kernel(g1_ref, gv_ref, gg_ref, ng_ref, out_ref, of_ref, ob_ref, sf_ref, sb_ref, v4_ref, *,
                n_ctx_chunks, n_chunks):
    c = SCAN_CHUNK
    key_masks = _key_lane_masks()
    state_mask = _state_mask()
    ti, si = _iota((c, c), 0), _iota((c, c), 1)
    causal = {True: jnp.where(si <= ti, 1.0, 0.0), False: jnp.where(si >= ti, 1.0, 0.0)}
    tri = {fwd: m.astype(BF16) for fwd, m in causal.items()}
    _fill_head_values(gv_ref, v4_ref, n_chunks * c)

    def chunk_fn(r0, fwd, s):
        q = g1_ref[0, pl.ds(r0, c), 0:128]
        k = g1_ref[0, pl.ds(r0, c), 128:256]
        a = g1_ref[0, pl.ds(r0, c), 256:384] if fwd else g1_ref[0, pl.ds(r0, c), 384:512]
        a_hi = a.astype(BF16)
        a_lo = (a - a_hi.astype(F32)).astype(BF16)
        b = (jnp.dot(tri[fwd], a_hi, preferred_element_type=F32)
             + jnp.dot(tri[fwd], a_lo, preferred_element_type=F32))
        b_mid = b[c // 2:c // 2 + 1, :]
        b_end = b[c - 1:c, :] if fwd else b[0:1, :]
        qe = (q * jnp.exp(b - b_mid)).astype(BF16)
        ke = (k * jnp.exp(b_mid - b)).astype(BF16)
        qdec = (q * jnp.exp(b)).astype(BF16)
        kdec = (k * jnp.exp(b_end - b)).astype(BF16)
        return _chunk_step(qe, ke, [causal[fwd]] * N_HEADS, qdec, kdec, jnp.exp(b_end), s,
                           gv_ref, v4_ref, r0, key_masks, state_mask)

    def token_scan():
        of_ref[...] = jnp.zeros_like(of_ref)
        ob_ref[...] = jnp.zeros_like(ob_ref)
        sf_ref[...] = jnp.zeros_like(sf_ref)
        sb_ref[...] = jnp.zeros_like(sb_ref)
        row = _iota((c, 1), 0)

        def token(r0, t, fwd):
            s_ref, o_ref = (sf_ref, of_ref) if fwd else (sb_ref, ob_ref)
            pick = row == t
            q = jnp.where(pick, g1_ref[0, pl.ds(r0, c), 0:128], 0.0).astype(BF16)
            k = jnp.where(pick, g1_ref[0, pl.ds(r0, c), 128:256], 0.0).astype(BF16)
            a = g1_ref[0, pl.ds(r0, c), 256:384] if fwd else g1_ref[0, pl.ds(r0, c), 384:512]
            a_t = jnp.sum(jnp.where(pick, a, 0.0), axis=0, keepdims=True)
            s = s_ref[...] * jnp.exp(a_t) + _dot_tn(gv_ref[0, pl.ds(r0, c), :], k) * state_mask
            s_ref[...] = s
            o_ref[pl.ds(r0, c), :] += _dot_nt(q, s.astype(BF16))

        def chunk_body(i, carry):
            rf = pl.multiple_of(i * c, c)
            cb = jnp.where(i < n_ctx_chunks, n_ctx_chunks - 1 - i, n_chunks + n_ctx_chunks - 1 - i)
            rb = pl.multiple_of(cb * c, c)

            def token_body(j, carry2):
                token(rf, j, True)
                token(rb, c - 1 - j, False)
                return carry2
            return lax.fori_loop(0, c, token_body, carry)
        lax.fori_loop(0, n_chunks, chunk_body, 0)

    def chunk_total(i, m):
        r0 = pl.multiple_of(i * c, c)
        return jnp.maximum(m, -jnp.sum(g1_ref[0, pl.ds(r0, c), 256:512], axis=0, keepdims=True))
    worst = jnp.max(lax.fori_loop(0, n_chunks, chunk_total, jnp.zeros((1, 2 * QK_W), F32)))
    chunked_ok = worst < GLA_MAX_CHUNK_LOG_DECAY

    @pl.when(chunked_ok)
    def _():
        _run_scan(chunk_fn, sf_ref, sb_ref, of_ref, ob_ref, n_ctx_chunks, n_chunks)

    @pl.when(jnp.logical_not(chunked_ok))
    def _():
        token_scan()

    ones_g = _group_ones()
    ng = ng_ref[...]

    for i in range((n_chunks * c) // TM):
        rows = slice(i * TM, (i + 1) * TM)
        o = of_ref[rows, :] + ob_ref[rows, :]
        ms = _dot_exact_rhs01(o * o, ones_g) * (1.0 / DV)
        y = o * lax.rsqrt(ms + EPS) * ng * _silu(gg_ref[0, rows, :])
        out_ref[0, rows, :] = y.astype(BF16)


def _ret_kernel(r1_ref, rv_ref, rg_ref, dec_head_ref, dec_lanes_ref, out_ref, of_ref, ob_ref, sf_ref,
                sb_ref, v4_ref, dm_ref, *, n_ctx_chunks, n_chunks):
    c = SCAN_CHUNK
    key_masks = _key_lane_masks()
    state_mask = _state_mask()
    ti, si = _iota((c, c), 0), _iota((c, c), 1)
    tl = _iota((c, QK_W), 0).astype(F32)
    _fill_head_values(rv_ref, v4_ref, n_chunks * c)

    consts = []
    for di in range(2):
        fwd = di == 0
        rel = (ti - si) if fwd else (si - ti)
        relf = jnp.maximum(rel, 0).astype(F32)
        for h in range(N_HEADS):
            lg = _log_sigmoid(dec_head_ref[di, h])
            dm_ref[di, h] = jnp.where(rel >= 0, jnp.exp(relf * lg), 0.0)
        lg_lanes = _log_sigmoid(dec_lanes_ref[di])
        n = tl if fwd else (c - 1.0) - tl
        consts.append((jnp.exp((n + 1.0) * lg_lanes), jnp.exp((c - 1.0 - n) * lg_lanes),
                       jnp.exp(float(c) * lg_lanes)))

    def chunk_fn(r0, fwd, s):
        di = 0 if fwd else 1
        q_dec, k_dec, s_dec = consts[di]
        q = r1_ref[0, pl.ds(r0, c), 0:128]
        k = r1_ref[0, pl.ds(r0, c), 128:256]
        return _chunk_step(q.astype(BF16), k.astype(BF16), [dm_ref[di, h] for h in range(N_HEADS)],
                           (q * q_dec).astype(BF16), (k * k_dec).astype(BF16), s_dec, s,
                           rv_ref, v4_ref, r0, key_masks, state_mask)

    _run_scan(chunk_fn, sf_ref, sb_ref, of_ref, ob_ref, n_ctx_chunks, n_chunks)

    ones_g = _group_ones()

    for i in range((n_chunks * c) // TM):
        rows = slice(i * TM, (i + 1) * TM)
        o = of_ref[rows, :] + ob_ref[rows, :]
        mu = _dot_exact_rhs01(o, ones_g) * (1.0 / DV)
        oc = o - mu
        var = _dot_exact_rhs01(oc * oc, ones_g) * (1.0 / DV)
        y = oc * lax.rsqrt(var + EPS) * _silu(rg_ref[0, rows, :])
        out_ref[0, rows, :] = y.astype(BF16)


def _scan_call(kernel, name, seq_inputs, small_inputs, extra_scratch, n_ctx):
    bsz, t_all, _ = seq_inputs[0].shape
    assert t_all % SCAN_CHUNK == 0 and n_ctx % SCAN_CHUNK == 0
    per_b = lambda a: pl.BlockSpec((1,) + a.shape[1:], lambda b: (b, 0, 0))
    full = lambda a: pl.BlockSpec(a.shape, lambda b: (0,) * a.ndim)
    state = pltpu.VMEM((V_W, QK_W), F32)
    o_dir = pltpu.VMEM((t_all, V_W), F32)
    return pl.pallas_call(
        functools.partial(kernel, n_ctx_chunks=n_ctx // SCAN_CHUNK, n_chunks=t_all // SCAN_CHUNK),
        grid=(bsz,),
        in_specs=[per_b(a) for a in seq_inputs] + [full(a) for a in small_inputs],
        out_specs=pl.BlockSpec((1, t_all, V_W), lambda b: (b, 0, 0)),
        out_shape=jax.ShapeDtypeStruct((bsz, t_all, V_W), BF16),
        scratch_shapes=[o_dir, o_dir, state, state, pltpu.VMEM((N_HEADS, t_all, V_W), BF16)] + extra_scratch,
        compiler_params=_cparams(("arbitrary",)),
        name=name,
    )(*seq_inputs, *small_inputs)


def _mla_kernel(q_ref, k_ref, vt_ref, x_ref, mg_ref, mr_ref, mod_ref, w_ref, g_ref, b_ref, o_ref, mm_ref, *,
                n_ctx, t_off):
    d = D_MODEL
    t = pl.program_id(1) + t_off
    vw = MLA_HEADS * MLA_DV

    def attend(n_keys):
        for pair in range(MLA_HEADS // 2):
            out_t = None
            for hh in range(2):
                lo = (2 * pair + hh) * MLA_HEAD_PAD
                st = _dot_nt(k_ref[0, 0:n_keys, lo:lo + MLA_HEAD_PAD], q_ref[0, :, lo:lo + MLA_HEAD_PAD])
                et = jnp.exp2(st - jnp.max(st, axis=0, keepdims=True))
                rinv = 1.0 / jnp.sum(et, axis=0, keepdims=True)
                vlo = hh * vw + pair * 128
                part = jnp.dot(vt_ref[0, vlo:vlo + 128, 0:n_keys], et.astype(BF16),
                               preferred_element_type=F32) * rinv
                out_t = part if out_t is None else out_t + part
            mm_ref[:, pair * 128:(pair + 1) * 128] = out_t.T.astype(BF16)

    def project():
        w = V_W
        y = (jnp.dot(mg_ref[0], w_ref[0:w, :], preferred_element_type=F32)
             + jnp.dot(mr_ref[0], w_ref[w:2 * w, :], preferred_element_type=F32)
             + jnp.dot(mm_ref[...], w_ref[2 * w:, :], preferred_element_type=F32))
        g1 = mod_ref[0][:, 2 * d:3 * d]
        o_ref[0] = _layer_norm(ALPHA * x_ref[0] + g1 * y, g_ref[...], b_ref[...])

    n_ctx_tiles = n_ctx // TM

    @pl.when(t < n_ctx_tiles)
    def _():
        attend(n_ctx)
        project()

    @pl.when(t >= n_ctx_tiles)
    def _():
        attend(k_ref.shape[1])
        project()


def _mla_call(mq, mk, mvt, xa, m_gla, m_ret, mod_l, wl, n_ctx, t_off):
    bsz, t_all, d = xa.shape
    nt = t_all // TM - t_off
    n_ctx_tiles = n_ctx // TM
    src = lambda w: pl.BlockSpec((1, TM, w), lambda b, t: (b, t + t_off, 0))
    full = lambda a: pl.BlockSpec(a.shape, lambda b, t: (0,) * a.ndim)
    return pl.pallas_call(
        functools.partial(_mla_kernel, n_ctx=n_ctx, t_off=t_off),
        grid=(bsz, nt),
        in_specs=[src(mq.shape[2]),
                  pl.BlockSpec((1, t_all, mk.shape[2]), lambda b, t: (b, 0, 0)),
                  pl.BlockSpec((1, mvt.shape[1], t_all), lambda b, t: (b, 0, 0)),
                  src(d), src(V_W), src(V_W),
                  pl.BlockSpec((1, 1, 6 * d),
                               lambda b, t: (2 * b + (t + t_off >= n_ctx_tiles).astype(jnp.int32), 0, 0)),
                  full(wl["w_out"]), full(wl["ln1_g"]), full(wl["ln1_b"])],
        out_specs=pl.BlockSpec((1, TM, d), lambda b, t: (b, t, 0)),
        out_shape=jax.ShapeDtypeStruct((bsz, nt * TM, d), F32),
        scratch_shapes=[pltpu.VMEM((TM, MLA_HEADS * MLA_DV), BF16)],
        compiler_params=_cparams(("arbitrary", "arbitrary")),
        name="mla_attn_out_ln1",
    )(mq, mk, mvt, xa, m_gla, m_ret, mod_l, wl["w_out"], wl["ln1_g"], wl["ln1_b"])


def _ffn_kernel(xp_ref, x_ref, xn_ref, mod_ref, up_ref, cw_ref, cb_ref, down_ref, g_ref, b_ref, o_ref,
                h2_ref, ua_ref, ug_ref, act_ref, *, sub, n_ctx_tiles, seq_starts, seq_ends):
    d = D_MODEL
    n_rows = TM + 2 * HALO
    n_ff = D_FF // FF_CHUNK
    t = pl.program_id(1)
    mod_ctx, mod_lat = mod_ref[0], mod_ref[1]

    def tile_mod(tile):
        if n_ctx_tiles == 0:
            pick = lambda k: mod_lat[:, k * d:(k + 1) * d]
        else:
            is_ctx = tile < n_ctx_tiles
            pick = lambda k: jnp.where(is_ctx, mod_ctx[:, k * d:(k + 1) * d], mod_lat[:, k * d:(k + 1) * d])
        return pick(3), pick(4), pick(5)

    sh_l, sc_l = mod_lat[:, 3 * d:4 * d], mod_lat[:, 4 * d:5 * d]
    h2_ref[0:HALO, :] = (xp_ref[0] * (1.0 + sc_l) + sh_l).astype(BF16)
    h2_ref[HALO + sub * TM:2 * HALO + sub * TM, :] = (xn_ref[0] * (1.0 + sc_l) + sh_l).astype(BF16)
    gates = []
    for s in range(sub):
        sh2, sc2, g2 = tile_mod(t * sub + s)
        gates.append(g2)
        h2_ref[HALO + s * TM:HALO + (s + 1) * TM, :] = (
            x_ref[0, s * TM:(s + 1) * TM, :] * (1.0 + sc2) + sh2).astype(BF16)

    def edge_keep(s):
        tile = t * sub + s
        is_start = functools.reduce(jnp.logical_or, [tile == v for v in seq_starts])
        is_end = functools.reduce(jnp.logical_or, [tile == v for v in seq_ends])
        return jnp.where(is_start, 0.0, 1.0), jnp.where(is_end, 0.0, 1.0)
    keeps = [edge_keep(s) for s in range(sub)]

    def up_stage(s, j, slot):
        keep_prev, keep_next = keeps[s]
        for u_ref, lo in ((ua_ref, j * FF_CHUNK), (ug_ref, D_FF + j * FF_CHUNK)):
            u = jnp.dot(h2_ref[s * TM:s * TM + n_rows, :], up_ref[:, lo:lo + FF_CHUNK],
                        preferred_element_type=F32)
            u_ref[slot, 0:HALO, :] = u[0:HALO] * keep_prev
            u_ref[slot, HALO:HALO + TM, :] = u[HALO:HALO + TM]
            u_ref[slot, HALO + TM:n_rows, :] = u[HALO + TM:n_rows] * keep_next

    def conv(u_ref, slot, lo):
        cw = cw_ref[:, lo:lo + FF_CHUNK]
        u = u_ref[slot]
        prev = pltpu.roll(u * cw[0:1], 1, axis=0)[HALO:HALO + TM]
        nxt = pltpu.roll(u * cw[2:3], n_rows - 1, axis=0)[HALO:HALO + TM]
        return prev + (u[HALO:HALO + TM] * cw[1:2] + cb_ref[:, lo:lo + FF_CHUNK]) + nxt

    def act_stage(s, j, slot):
        a = conv(ua_ref, slot, j * FF_CHUNK)
        gt = conv(ug_ref, slot, D_FF + j * FF_CHUNK)
        act_ref[s % 2, :, j * FF_CHUNK:(j + 1) * FF_CHUNK] = (_silu(a) * gt).astype(BF16)

    items = [(s, j) for s in range(sub) for j in range(n_ff)]
    acc = None
    up_stage(0, 0, 0)
    for k, (s, j) in enumerate(items):
        if k + 1 < len(items):
            up_stage(*items[k + 1], (k + 1) % 2)
        act_stage(s, j, k % 2)
        down = lambda lo, hi: jnp.dot(act_ref[s % 2, :, lo:hi], down_ref[lo:hi, :], preferred_element_type=F32)
        if (j + 1) % DOWN_EVERY == 0 or j == n_ff - 1:
            lo = (j // DOWN_EVERY) * DOWN_EVERY * FF_CHUNK
            part = down(lo, (j + 1) * FF_CHUNK)
            acc = part if lo == 0 else acc + part
        if j == n_ff - 1:
            x1 = x_ref[0, s * TM:(s + 1) * TM, :]
            o_ref[0, s * TM:(s + 1) * TM, :] = _layer_norm(ALPHA * x1 + gates[s] * acc, g_ref[...], b_ref[...])


def _ffn_call(x1, mod_l, wl, n_ctx_tiles, t_off):
    bsz, rows, d = x1.shape
    nt = rows // TM
    sub = next(s for s in FFN_TILES_PER_BLOCK if nt % s == 0)
    br = sub * TM
    hb = br // HALO
    n_hblk = rows // HALO
    n_ctx_local = max(n_ctx_tiles - t_off, 0)
    starts = sorted({max(0 - t_off, 0), n_ctx_local})
    ends = sorted({e for e in (n_ctx_local - 1, nt - 1) if e >= 0})
    full = lambda a: pl.BlockSpec(a.shape, lambda b, t: (0,) * a.ndim, pipeline_mode=pl.Buffered(1))
    weights = (wl["ffn_up"], wl["conv_w"], wl["conv_b"], wl["ffn_down"], wl["ln2_g"], wl["ln2_b"])
    n_rows = TM + 2 * HALO
    return pl.pallas_call(
        functools.partial(_ffn_kernel, sub=sub, n_ctx_tiles=n_ctx_local, seq_starts=tuple(starts),
                          seq_ends=tuple(ends)),
        grid=(bsz, nt // sub),
        in_specs=[pl.BlockSpec((1, HALO, d), lambda b, t: (b, jnp.maximum(t * hb - 1, 0), 0)),
                  pl.BlockSpec((1, br, d), lambda b, t: (b, t, 0)),
                  pl.BlockSpec((1, HALO, d), lambda b, t: (b, jnp.minimum((t + 1) * hb, n_hblk - 1), 0)),
                  pl.BlockSpec((2, 1, 6 * d), lambda b, t: (b, 0, 0))]
                 + [full(w) for w in weights],
        out_specs=pl.BlockSpec((1, br, d), lambda b, t: (b, t, 0)),
        out_shape=jax.ShapeDtypeStruct((bsz, rows, d), F32),
        scratch_shapes=[pltpu.VMEM((br + 2 * HALO, d), BF16),
                        pltpu.VMEM((2, n_rows, FF_CHUNK), F32), pltpu.VMEM((2, n_rows, FF_CHUNK), F32),
                        pltpu.VMEM((2, TM, D_FF), BF16)],
        compiler_params=_cparams(("arbitrary", "arbitrary")),
        name="conv_ffn_ln2",
    )(x1, x1, x1, mod_l, *weights)


def _rope_tables(seq, n_ctx):
    pos = jnp.arange(seq, dtype=F32)
    ret_inv = 1.0 / (ROPE_BASE ** jnp.linspace(0.0, 1.0, DK // 2, dtype=F32))
    ret_ang = pos[:, None] * ret_inv
    rc, rs = jnp.cos(ret_ang), jnp.sin(ret_ang)
    ret_cos = jnp.tile(jnp.concatenate([rc, rc], axis=1), (1, N_HEADS))
    ret_sin = jnp.tile(jnp.concatenate([-rs, rs], axis=1), (1, N_HEADS))

    rows = jnp.repeat(jnp.arange(seq // GRID_W, dtype=F32), GRID_W)
    cols = jnp.tile(jnp.arange(GRID_W, dtype=F32), seq // GRID_W)
    n_ax = MLA_D_ROPE // 4
    ax_inv = ROPE_BASE ** (-jnp.arange(n_ax, dtype=F32) / n_ax)
    ra, ca = rows[:, None] * ax_inv, cols[:, None] * ax_inv
    cos32 = jnp.concatenate([jnp.cos(ra), jnp.cos(ra), jnp.cos(ca), jnp.cos(ca)], axis=1)
    sin32 = jnp.concatenate([-jnp.sin(ra), jnp.sin(ra), -jnp.sin(ca), jnp.sin(ca)], axis=1)
    one, zero = jnp.ones((seq, MLA_D_NOPE), F32), jnp.zeros((seq, 32), F32)
    q_cos = MLA_QSCALE * jnp.concatenate([one, cos32, zero], axis=1)
    q_sin = MLA_QSCALE * jnp.concatenate([0.0 * one, sin32, zero], axis=1)
    misc = jnp.concatenate([cos32, sin32, zero, zero], axis=1)
    lat = jnp.concatenate([ret_cos, ret_sin, q_cos, q_sin, misc], axis=1)

    c1, c0 = jnp.ones((n_ctx, 128), F32), jnp.zeros((n_ctx, 128), F32)
    q_cos_c = MLA_QSCALE * jnp.concatenate([jnp.ones((n_ctx, 96), F32), jnp.zeros((n_ctx, 32), F32)], axis=1)
    misc_c = jnp.concatenate([jnp.ones((n_ctx, 32), F32), jnp.zeros((n_ctx, 96), F32)], axis=1)
    ctx = jnp.concatenate([c1, c0, q_cos_c, c0, misc_c], axis=1)
    return jnp.concatenate([ctx, lat], axis=0)


def _swap_idx(n, half):
    j = np.arange(n)
    return (j // (2 * half)) * (2 * half) + (j % (2 * half) + half) % (2 * half)


def _layer_weights(i, w_in, gla_gate_w, gla_gate_b, gla_norm_g, ret_decay, mla_q_norm_g, mla_kv_norm_g,
                   mla_w_uq, mla_w_uk, mla_w_uv, w_out, ln1_g, ln1_b, ffn_up, ffn_conv_w, ffn_conv_b,
                   ffn_down, ln2_g, ln2_b):
    d = D_MODEL
    o = IN_OFFS
    wi = w_in[i]
    sec = lambda k: wi[:, o[k]:o[k + 1]]
    swap_ret = _swap_idx(QK_W, DK // 2)
    swap_kr = _swap_idx(MLA_D_ROPE, MLA_D_ROPE // 4)
    kr = sec(11)
    misc = jnp.concatenate([kr, kr[:, swap_kr], sec(3), jnp.zeros((d, 32), F32)], axis=1)
    w_in_p = jnp.concatenate([sec(0), sec(1), sec(2), sec(4), sec(5), sec(6), sec(7), sec(8), sec(9),
                              sec(10), misc, sec(5)[:, swap_ret], sec(6)[:, swap_ret]], axis=1).astype(BF16)

    wg = jnp.zeros((128, 2 * QK_W), F32)
    wg = wg.at[64:64 + GATE_RANK, 0:QK_W].set(gla_gate_w[i, 0])
    wg = wg.at[64 + GATE_RANK:64 + 2 * GATE_RANK, QK_W:].set(gla_gate_w[i, 1])
    bg = jnp.concatenate([gla_gate_b[i, 0], gla_gate_b[i, 1]])[None, :]

    uq = mla_w_uq[i].reshape(MLA_Q_RANK, MLA_HEADS, MLA_D_NOPE + MLA_D_ROPE)
    z32 = jnp.zeros((MLA_Q_RANK, MLA_HEADS, 32), F32)
    z64 = jnp.zeros((MLA_Q_RANK, MLA_HEADS, 64), F32)
    uq_main = jnp.concatenate([uq, z32], axis=2).reshape(MLA_Q_RANK, -1)
    uq_sw = jnp.concatenate([z64, uq[:, :, MLA_D_NOPE:][:, :, swap_kr], z32], axis=2).reshape(MLA_Q_RANK, -1)
    w_uq = jnp.concatenate([uq_main, uq_sw], axis=1).astype(BF16)

    uk = mla_w_uk[i].reshape(MLA_KV_RANK, MLA_HEADS, MLA_D_NOPE)
    uk_pad = jnp.concatenate([uk, jnp.zeros((MLA_KV_RANK, MLA_HEADS, 64), F32)], axis=2).reshape(MLA_KV_RANK, -1)
    uv = mla_w_uv[i].reshape(MLA_KV_RANK, MLA_HEADS, MLA_DV)
    even = (jnp.arange(MLA_HEADS) % 2 == 0)[None, :, None]
    uv_a = jnp.where(even, uv, 0.0).reshape(MLA_KV_RANK, -1)
    uv_b = jnp.where(even, 0.0, uv).reshape(MLA_KV_RANK, -1)
    place = jnp.zeros((128, MLA_HEADS, MLA_HEAD_PAD), F32)
    eye = jnp.eye(MLA_D_ROPE, dtype=F32)
    place = place.at[0:MLA_D_ROPE, :, MLA_D_NOPE:MLA_D_NOPE + MLA_D_ROPE].set(
        jnp.broadcast_to(eye[:, None, :], (MLA_D_ROPE, MLA_HEADS, MLA_D_ROPE)))
    place = place.reshape(128, -1)
    w_kv = jnp.concatenate([
        jnp.concatenate([uk_pad, uv_a, uv_b], axis=1),
        jnp.concatenate([place, jnp.zeros((128, 2 * MLA_HEADS * MLA_DV), F32)], axis=1)], axis=0).astype(BF16)

    row = lambda a: a[None, :]
    return {
        "w_in": w_in_p, "w_gate": wg.astype(BF16), "b_gate": bg,
        "qn_g": row(mla_q_norm_g[i]), "kvn_g": row(mla_kv_norm_g[i]), "w_uq": w_uq, "w_kv": w_kv,
        "gla_ng": row(jnp.tile(gla_norm_g[i], N_HEADS)),
        "dec_head": jnp.broadcast_to(ret_decay[i][:, :, None, None], (2, N_HEADS, 1, SCAN_CHUNK)),
        "dec_lanes": jnp.repeat(ret_decay[i], DK, axis=1)[:, None, :],
        "w_out": w_out[i].astype(BF16), "ln1_g": row(ln1_g[i]), "ln1_b": row(ln1_b[i]),
        "ffn_up": ffn_up[i].astype(BF16), "conv_w": ffn_conv_w[i], "conv_b": row(ffn_conv_b[i]),
        "ffn_down": ffn_down[i].astype(BF16), "ln2_g": row(ln2_g[i]), "ln2_b": row(ln2_b[i]),
    }


def kernel(x, c, ctx, c_ctx, ada_w, ada_b, w_in, gla_gate_w, gla_gate_b, gla_norm_g, ret_decay, mla_q_norm_g, mla_kv_norm_g, mla_w_uq, mla_w_uk, mla_w_uv, w_out, ln1_g, ln1_b, ffn_up, ffn_conv_w, ffn_conv_b, ffn_down, ln2_g, ln2_b):
    bsz, seq, d = x.shape
    n_ctx = ctx.shape[1]
    assert d == D_MODEL and seq % TM == 0 and n_ctx % TM == 0 and seq % GRID_W == 0
    n_ctx_tiles = n_ctx // TM

    rows = -(-(bsz + 1) // 8) * 8
    c_all = jnp.concatenate([c, c_ctx[None, :], jnp.zeros((rows - bsz - 1, d), F32)], axis=0)
    mod = _ada_call(c_all, ada_w, ada_b)
    mod_ctx = jnp.broadcast_to(mod[:, bsz:bsz + 1], (DEPTH, bsz, 6 * d))
    mod_sel = jnp.stack([mod_ctx, mod[:, :bsz]], axis=2).reshape(DEPTH, 2 * bsz, 1, 6 * d)

    tab = _rope_tables(seq, n_ctx)
    xa = jnp.concatenate([ctx, x], axis=1)
    dmat_scratch = [pltpu.VMEM((2, N_HEADS, SCAN_CHUNK, SCAN_CHUNK), F32)]

    for i in range(DEPTH):
        wl = _layer_weights(i, w_in, gla_gate_w, gla_gate_b, gla_norm_g, ret_decay, mla_q_norm_g,
                            mla_kv_norm_g, mla_w_uq, mla_w_uk, mla_w_uv, w_out, ln1_g, ln1_b, ffn_up,
                            ffn_conv_w, ffn_conv_b, ffn_down, ln2_g, ln2_b)
        mod_l = mod_sel[i]
        need_ctx = i < DEPTH - 1
        t_off = 0 if need_ctx else n_ctx_tiles
        g1, gv, gg, r1, rv, rg, mq, mk, mvt = _proj_call(xa, mod_l, tab, wl, n_ctx_tiles)
        m_gla = _scan_call(_gla_kernel, "gla_scan", (g1, gv, gg), (wl["gla_ng"],), [], n_ctx)
        m_ret = _scan_call(_ret_kernel, "ret_scan", (r1, rv, rg), (wl["dec_head"], wl["dec_lanes"]),
                           dmat_scratch, n_ctx)
        x1 = _mla_call(mq, mk, mvt, xa, m_gla, m_ret, mod_l, wl, n_ctx, t_off)
        xa = _ffn_call(x1, mod_l, wl, n_ctx_tiles, t_off)
    return xa
```

```python
import functools

import numpy as np
import jax
import jax.numpy as jnp
from jax import lax
from jax.experimental import pallas as pl
from jax.experimental.pallas import tpu as pltpu

F32 = jnp.float32
BF16 = jnp.bfloat16

D_MODEL = 1024
DEPTH = 4
GRID_W = 64
N_HEADS = 4
DK = 32
DV = 64
QK_W = N_HEADS * DK
V_W = N_HEADS * DV
GATE_RANK = 16
GLA_TAU = 16.0
MLA_HEADS = 8
MLA_D_NOPE = 64
MLA_D_ROPE = 32
MLA_DV = 64
MLA_Q_RANK = 256
MLA_KV_RANK = 128
MLA_HEAD_PAD = 128
MLA_SCALE = (MLA_D_NOPE + MLA_D_ROPE) ** -0.5
MLA_QSCALE = MLA_SCALE * float(np.log2(np.e))
D_FF = 2816
ROPE_BASE = 10000.0
EPS = 1e-6
ALPHA = (2 * DEPTH) ** 0.25
IN_SIZES = (QK_W, QK_W, V_W, 2 * GATE_RANK, V_W, QK_W, QK_W, V_W, V_W,
            MLA_Q_RANK, MLA_KV_RANK, MLA_D_ROPE)
IN_OFFS = tuple(int(o) for o in np.concatenate([[0], np.cumsum(IN_SIZES)]))

P_GQ, P_GK, P_GV, P_GG = 0, 128, 256, 512
P_RQ, P_RK, P_RV, P_RG = 768, 896, 1024, 1280
P_CQ, P_CKV, P_MISC = 1536, 1792, 1920
P_RQS, P_RKS = 2048, 2176
P_W = 2304

TM = 256
SCAN_CHUNK = 256
GLA_MAX_CHUNK_LOG_DECAY = 64.0
HALO = 8
FF_CHUNK = 256
FFN_TILES_PER_BLOCK = (3, 4, 2, 1)
DOWN_EVERY = 6
VMEM_LIMIT = 56 * 1024 * 1024


def _cparams(sem):
    return pltpu.CompilerParams(dimension_semantics=sem, vmem_limit_bytes=VMEM_LIMIT)


def _iota(shape, dim):
    return lax.broadcasted_iota(jnp.int32, shape, dim)


def _split3(x):
    hi = x.astype(BF16)
    r1 = x - hi.astype(F32)
    mid = r1.astype(BF16)
    lo = (r1 - mid.astype(F32)).astype(BF16)
    return hi, mid, lo


def _dot_exact_rhs01(x, m01):
    hi, mid, lo = _split3(x)
    d = lambda t: jnp.dot(t, m01, preferred_element_type=F32)
    return d(hi) + d(mid) + d(lo)


def _log_sigmoid(z):
    return jnp.minimum(z, 0.0) - jnp.log1p(jnp.exp(-jnp.abs(z)))


def _silu(x):
    return x * jax.nn.sigmoid(x)


def _layer_norm(x, g, b):
    mu = jnp.mean(x, axis=-1, keepdims=True)
    xc = x - mu
    var = jnp.mean(xc * xc, axis=-1, keepdims=True)
    return xc * lax.rsqrt(var + EPS) * g + b


def _rms_norm(x, g):
    return x * lax.rsqrt(jnp.mean(x * x, axis=-1, keepdims=True) + EPS) * g


def _dot_nt(a, b):
    return lax.dot_general(a, b, (((1,), (1,)), ((), ())), preferred_element_type=F32)


def _dot_tn(a, b):
    return lax.dot_general(a, b, (((0,), (0,)), ((), ())), preferred_element_type=F32)


def _ada_kernel(c_ref, w_ref, b_ref, o_ref):
    a = _silu(c_ref[...]).astype(BF16)
    o_ref[0] = jnp.dot(a, w_ref[0].astype(BF16), preferred_element_type=F32) + b_ref[0]


def _ada_call(c_all, ada_w, ada_b):
    depth, d, e = ada_w.shape
    rows = c_all.shape[0]
    bn = 1024
    return pl.pallas_call(
        _ada_kernel,
        grid=(depth, e // bn),
        in_specs=[pl.BlockSpec((rows, d), lambda l, n: (0, 0)),
                  pl.BlockSpec((1, d, bn), lambda l, n: (l, 0, n)),
                  pl.BlockSpec((1, 1, bn), lambda l, n: (l, 0, n))],
        out_specs=pl.BlockSpec((1, rows, bn), lambda l, n: (l, 0, n)),
        out_shape=jax.ShapeDtypeStruct((depth, rows, e), F32),
        compiler_params=_cparams(("arbitrary", "arbitrary")),
        name="ada_mod",
    )(c_all, ada_w, ada_b.reshape(depth, 1, e))


def _proj_kernel(x_ref, mod_ref, tab_ref, w_in_ref, w_gate_ref, b_gate_ref, qn_g_ref, kvn_g_ref,
                 w_uq_ref, w_kv_ref,
                 g1_ref, gv_ref, gg_ref, r1_ref, rv_ref, rg_ref, mq_ref, mkv_ref):
    d = D_MODEL
    x = x_ref[0]
    mod = mod_ref[0]
    h = (x * (1.0 + mod[:, d:2 * d]) + mod[:, 0:d]).astype(BF16)
    p = jnp.dot(h, w_in_ref[...], preferred_element_type=F32)

    ret_cos = tab_ref[:, 0:128]
    ret_sin = tab_ref[:, 128:256]
    q_cos = tab_ref[:, 256:384]
    q_sin = tab_ref[:, 384:512]
    misc_tab = tab_ref[:, 512:640]

    misc = p[:, P_MISC:P_MISC + 128]
    z = jnp.dot(misc.astype(BF16), w_gate_ref[...], preferred_element_type=F32) + b_gate_ref[...]
    log_a = _log_sigmoid(z) / GLA_TAU
    g1_ref[0, :, 0:128] = p[:, P_GQ:P_GQ + 128] * (DK ** -0.5)
    g1_ref[0, :, 128:256] = p[:, P_GK:P_GK + 128]
    g1_ref[0, :, 256:512] = log_a
    gv_ref[0] = p[:, P_GV:P_GV + V_W].astype(BF16)
    gg_ref[0] = p[:, P_GG:P_GG + V_W]

    r1_ref[0, :, 0:128] = p[:, P_RQ:P_RQ + 128] * ret_cos + p[:, P_RQS:P_RQS + 128] * ret_sin
    ks = DK ** -0.5
    r1_ref[0, :, 128:256] = ((p[:, P_RK:P_RK + 128] * ks) * ret_cos
                             + (p[:, P_RKS:P_RKS + 128] * ks) * ret_sin)
    rv_ref[0] = p[:, P_RV:P_RV + V_W].astype(BF16)
    rg_ref[0] = p[:, P_RG:P_RG + V_W]

    cq = _rms_norm(p[:, P_CQ:P_CQ + MLA_Q_RANK], qn_g_ref[...]).astype(BF16)
    qq = jnp.dot(cq, w_uq_ref[...], preferred_element_type=F32)
    qw = MLA_HEADS * MLA_HEAD_PAD
    for hh in range(MLA_HEADS):
        lo = hh * MLA_HEAD_PAD
        mq_ref[0, :, lo:lo + MLA_HEAD_PAD] = (
            qq[:, lo:lo + MLA_HEAD_PAD] * q_cos + qq[:, qw + lo:qw + lo + MLA_HEAD_PAD] * q_sin
        ).astype(BF16)

    ckv = _rms_norm(p[:, P_CKV:P_CKV + MLA_KV_RANK], kvn_g_ref[...])
    prod = misc * misc_tab
    kr = prod + pltpu.roll(prod, 128 - MLA_D_ROPE, axis=1)
    lhs = jnp.concatenate([ckv, kr], axis=1).astype(BF16)
    mkv_ref[0] = jnp.dot(lhs, w_kv_ref[...], preferred_element_type=F32).astype(BF16)


def _proj_call(xa, mod_l, tab, wl, n_ctx_tiles):
    bsz, t_all, d = xa.shape
    nt = t_all // TM
    tile = lambda w: pl.BlockSpec((1, TM, w), lambda b, t: (b, t, 0))
    full = lambda a: pl.BlockSpec(a.shape, lambda b, t: (0,) * a.ndim)
    sds = lambda w, dt: jax.ShapeDtypeStruct((bsz, t_all, w), dt)
    weights = (wl["w_in"], wl["w_gate"], wl["b_gate"], wl["qn_g"], wl["kvn_g"], wl["w_uq"], wl["w_kv"])
    return pl.pallas_call(
        _proj_kernel,
        grid=(bsz, nt),
        in_specs=[tile(d),
                  pl.BlockSpec((1, 1, 6 * d), lambda b, t: (2 * b + (t >= n_ctx_tiles).astype(jnp.int32), 0, 0)),
                  pl.BlockSpec((TM, tab.shape[1]), lambda b, t: (t, 0))]
                 + [full(w) for w in weights],
        out_specs=[tile(512), tile(V_W), tile(V_W), tile(256), tile(V_W), tile(V_W),
                   tile(MLA_HEADS * MLA_HEAD_PAD), tile(2048)],
        out_shape=[sds(512, F32), sds(V_W, BF16), sds(V_W, F32), sds(256, F32), sds(V_W, BF16),
                   sds(V_W, F32), sds(MLA_HEADS * MLA_HEAD_PAD, BF16), sds(2048, BF16)],
        compiler_params=_cparams(("arbitrary", "arbitrary")),
        name="proj",
    )(xa, mod_l, tab, *weights)


def _key_lane_masks():
    lane_head = _iota((1, QK_W), 1) // DK
    return [jnp.where(lane_head == h, 1.0, 0.0).astype(BF16) for h in range(N_HEADS)]


def _state_mask():
    return jnp.where((_iota((V_W, QK_W), 0) // DV) == (_iota((V_W, QK_W), 1) // DK), 1.0, 0.0)


def _fill_head_values(v_ref, v4_ref, n_rows):
    lane_head = _iota((TM, V_W), 1) // DV

    def body(i, carry):
        r0 = pl.multiple_of(i * TM, TM)
        v = v_ref[0, pl.ds(r0, TM), :].astype(F32)
        for h in range(N_HEADS):
            v4_ref[h, pl.ds(r0, TM), :] = jnp.where(lane_head == h, v, 0.0).astype(BF16)
        return carry
    lax.fori_loop(0, n_rows // TM, body, 0)


def _chunk_step(qe, ke, weights, qdec, kdec, sdec, s, v_ref, v4_ref, r0, key_masks, state_mask):
    c = SCAN_CHUNK
    o = None
    for h in range(N_HEADS):
        att = _dot_nt(qe * key_masks[h], ke) * weights[h]
        part = jnp.dot(att.astype(BF16), v4_ref[h, pl.ds(r0, c), :], preferred_element_type=F32)
        o = part if o is None else o + part
    o = o + _dot_nt(qdec, s.astype(BF16))
    upd = _dot_tn(v_ref[0, pl.ds(r0, c), :], kdec)
    return o, s * sdec + upd * state_mask


def _run_scan(chunk_fn, sf_ref, sb_ref, of_ref, ob_ref, n_ctx_chunks, n_chunks):
    s_f = jnp.zeros(sf_ref.shape, F32)
    s_b = jnp.zeros(sb_ref.shape, F32)
    for i in range(n_chunks):
        cb = n_ctx_chunks - 1 - i if i < n_ctx_chunks else n_chunks + n_ctx_chunks - 1 - i
        o_f, s_f = chunk_fn(i * SCAN_CHUNK, True, s_f)
        o_b, s_b = chunk_fn(cb * SCAN_CHUNK, False, s_b)
        of_ref[i * SCAN_CHUNK:(i + 1) * SCAN_CHUNK, :] = o_f
        ob_ref[cb * SCAN_CHUNK:(cb + 1) * SCAN_CHUNK, :] = o_b


def _group_ones():
    same = (_iota((V_W, V_W), 0) // DV) == (_iota((V_W, V_W), 1) // DV)
    return jnp.where(same, 1.0, 0.0).astype(BF16)


def _gla_kernel(g1_ref, gv_ref, gg_ref, ng_ref, out_ref, of_ref, ob_ref, sf_ref, sb_ref, v4_ref, *,
                n_ctx_chunks, n_chunks):
    c = SCAN_CHUNK
    key_masks = _key_lane_masks()
    state_mask = _state_mask()
    ti, si = _iota((c, c), 0), _iota((c, c), 1)
    causal = {True: jnp.where(si <= ti, 1.0, 0.0), False: jnp.where(si >= ti, 1.0, 0.0)}
    tri = {fwd: m.astype(BF16) for fwd, m in causal.items()}
    _fill_head_values(gv_ref, v4_ref, n_chunks * c)

    def chunk_fn(r0, fwd, s):
        q = g1_ref[0, pl.ds(r0, c), 0:128]
        k = g1_ref[0, pl.ds(r0, c), 128:256]
        a = g1_ref[0, pl.ds(r0, c), 256:384] if fwd else g1_ref[0, pl.ds(r0, c), 384:512]
        a_hi = a.astype(BF16)
        a_lo = (a - a_hi.astype(F32)).astype(BF16)
        b = (jnp.dot(tri[fwd], a_hi, preferred_element_type=F32)
             + jnp.dot(tri[fwd], a_lo, preferred_element_type=F32))
        b_mid = b[c // 2:c // 2 + 1, :]
        b_end = b[c - 1:c, :] if fwd else b[0:1, :]
        qe = (q * jnp.exp(b - b_mid)).astype(BF16)
        ke = (k * jnp.exp(b_mid - b)).astype(BF16)
        qdec = (q * jnp.exp(b)).astype(BF16)
        kdec = (k * jnp.exp(b_end - b)).astype(BF16)
        return _chunk_step(qe, ke, [causal[fwd]] * N_HEADS, qdec, kdec, jnp.exp(b_end), s,
                           gv_ref, v4_ref, r0, key_masks, state_mask)

    def token_scan():
        of_ref[...] = jnp.zeros_like(of_ref)
        ob_ref[...] = jnp.zeros_like(ob_ref)
        sf_ref[...] = jnp.zeros_like(sf_ref)
        sb_ref[...] = jnp.zeros_like(sb_ref)
        row = _iota((c, 1), 0)

        def token(r0, t, fwd):
            s_ref, o_ref = (sf_ref, of_ref) if fwd else (sb_ref, ob_ref)
            pick = row == t
            q = jnp.where(pick, g1_ref[0, pl.ds(r0, c), 0:128], 0.0).astype(BF16)
            k = jnp.where(pick, g1_ref[0, pl.ds(r0, c), 128:256], 0.0).astype(BF16)
            a = g1_ref[0, pl.ds(r0, c), 256:384] if fwd else g1_ref[0, pl.ds(r0, c), 384:512]
            a_t = jnp.sum(jnp.where(pick, a, 0.0), axis=0, keepdims=True)
            s = s_ref[...] * jnp.exp(a_t) + _dot_tn(gv_ref[0, pl.ds(r0, c), :], k) * state_mask
            s_ref[...] = s
            o_ref[pl.ds(r0, c), :] += _dot_nt(q, s.astype(BF16))

        def chunk_body(i, carry):
            rf = pl.multiple_of(i * c, c)
            cb = jnp.where(i < n_ctx_chunks, n_ctx_chunks - 1 - i, n_chunks + n_ctx_chunks - 1 - i)
            rb = pl.multiple_of(cb * c, c)

            def token_body(j, carry2):
                token(rf, j, True)
                token(rb, c - 1 - j, False)
                return carry2
            return lax.fori_loop(0, c, token_body, carry)
        lax.fori_loop(0, n_chunks, chunk_body, 0)

    def chunk_total(i, m):
        r0 = pl.multiple_of(i * c, c)
        return jnp.maximum(m, -jnp.sum(g1_ref[0, pl.ds(r0, c), 256:512], axis=0, keepdims=True))
    worst = jnp.max(lax.fori_loop(0, n_chunks, chunk_total, jnp.zeros((1, 2 * QK_W), F32)))
    chunked_ok = worst < GLA_MAX_CHUNK_LOG_DECAY

    @pl.when(chunked_ok)
    def _():
        _run_scan(chunk_fn, sf_ref, sb_ref, of_ref, ob_ref, n_ctx_chunks, n_chunks)

    @pl.when(jnp.logical_not(chunked_ok))
    def _():
        token_scan()

    ones_g = _group_ones()
    ng = ng_ref[...]

    for i in range((n_chunks * c) // TM):
        rows = slice(i * TM, (i + 1) * TM)
        o = of_ref[rows, :] + ob_ref[rows, :]
        ms = _dot_exact_rhs01(o * o, ones_g) * (1.0 / DV)
        y = o * lax.rsqrt(ms + EPS) * ng * _silu(gg_ref[0, rows, :])
        out_ref[0, rows, :] = y.astype(BF16)


def _ret_kernel(r1_ref, rv_ref, rg_ref, dec_head_ref, dec_lanes_ref, out_ref, of_ref, ob_ref, sf_ref,
                sb_ref, v4_ref, dm_ref, *, n_ctx_chunks, n_chunks):
    c = SCAN_CHUNK
    key_masks = _key_lane_masks()
    state_mask = _state_mask()
    ti, si = _iota((c, c), 0), _iota((c, c), 1)
    tl = _iota((c, QK_W), 0).astype(F32)
    _fill_head_values(rv_ref, v4_ref, n_chunks * c)

    consts = []
    for di in range(2):
        fwd = di == 0
        rel = (ti - si) if fwd else (si - ti)
        relf = jnp.maximum(rel, 0).astype(F32)
        for h in range(N_HEADS):
            lg = _log_sigmoid(dec_head_ref[di, h])
            dm_ref[di, h] = jnp.where(rel >= 0, jnp.exp(relf * lg), 0.0)
        lg_lanes = _log_sigmoid(dec_lanes_ref[di])
        n = tl if fwd else (c - 1.0) - tl
        consts.append((jnp.exp((n + 1.0) * lg_lanes), jnp.exp((c - 1.0 - n) * lg_lanes),
                       jnp.exp(float(c) * lg_lanes)))

    def chunk_fn(r0, fwd, s):
        di = 0 if fwd else 1
        q_dec, k_dec, s_dec = consts[di]
        q = r1_ref[0, pl.ds(r0, c), 0:128]
        k = r1_ref[0, pl.ds(r0, c), 128:256]
        return _chunk_step(q.astype(BF16), k.astype(BF16), [dm_ref[di, h] for h in range(N_HEADS)],
                           (q * q_dec).astype(BF16), (k * k_dec).astype(BF16), s_dec, s,
                           rv_ref, v4_ref, r0, key_masks, state_mask)

    _run_scan(chunk_fn, sf_ref, sb_ref, of_ref, ob_ref, n_ctx_chunks, n_chunks)

    ones_g = _group_ones()

    for i in range((n_chunks * c) // TM):
        rows = slice(i * TM, (i + 1) * TM)
        o = of_ref[rows, :] + ob_ref[rows, :]
        mu = _dot_exact_rhs01(o, ones_g) * (1.0 / DV)
        oc = o - mu
        var = _dot_exact_rhs01(oc * oc, ones_g) * (1.0 / DV)
        y = oc * lax.rsqrt(var + EPS) * _silu(rg_ref[0, rows, :])
        out_ref[0, rows, :] = y.astype(BF16)


def _scan_call(kernel, name, seq_inputs, small_inputs, extra_scratch, n_ctx):
    bsz, t_all, _ = seq_inputs[0].shape
    assert t_all % SCAN_CHUNK == 0 and n_ctx % SCAN_CHUNK == 0
    per_b = lambda a: pl.BlockSpec((1,) + a.shape[1:], lambda b: (b, 0, 0))
    full = lambda a: pl.BlockSpec(a.shape, lambda b: (0,) * a.ndim)
    state = pltpu.VMEM((V_W, QK_W), F32)
    o_dir = pltpu.VMEM((t_all, V_W), F32)
    return pl.pallas_call(
        functools.partial(kernel, n_ctx_chunks=n_ctx // SCAN_CHUNK, n_chunks=t_all // SCAN_CHUNK),
        grid=(bsz,),
        in_specs=[per_b(a) for a in seq_inputs] + [full(a) for a in small_inputs],
        out_specs=pl.BlockSpec((1, t_all, V_W), lambda b: (b, 0, 0)),
        out_shape=jax.ShapeDtypeStruct((bsz, t_all, V_W), BF16),
        scratch_shapes=[o_dir, o_dir, state, state, pltpu.VMEM((N_HEADS, t_all, V_W), BF16)] + extra_scratch,
        compiler_params=_cparams(("arbitrary",)),
        name=name,
    )(*seq_inputs, *small_inputs)


def _mla_kernel(q_ref, kv_ref, x_ref, mg_ref, mr_ref, mod_ref, w_ref, g_ref, b_ref, o_ref, mm_ref, *,
                n_ctx, t_off):
    d = D_MODEL
    t = pl.program_id(1) + t_off
    kw = MLA_HEADS * MLA_HEAD_PAD
    vw = MLA_HEADS * MLA_DV
    first_half = _iota((1, 128), 1) < MLA_DV

    def attend(n_keys):
        for pair in range(MLA_HEADS // 2):
            pv, rinv = [], []
            for hh in range(2):
                lo = (2 * pair + hh) * MLA_HEAD_PAD
                s = _dot_nt(q_ref[0, :, lo:lo + MLA_HEAD_PAD], kv_ref[0, 0:n_keys, lo:lo + MLA_HEAD_PAD])
                e = jnp.exp2(s - jnp.max(s, axis=-1, keepdims=True))
                rinv.append(1.0 / jnp.sum(e, axis=-1, keepdims=True))
                vlo = kw + hh * vw + pair * 128
                pv.append(jnp.dot(e.astype(BF16), kv_ref[0, 0:n_keys, vlo:vlo + 128],
                                  preferred_element_type=F32))
            scale = jnp.where(first_half, rinv[0], rinv[1])
            mm_ref[:, pair * 128:(pair + 1) * 128] = ((pv[0] + pv[1]) * scale).astype(BF16)

    def project():
        w = V_W
        y = (jnp.dot(mg_ref[0], w_ref[0:w, :], preferred_element_type=F32)
             + jnp.dot(mr_ref[0], w_ref[w:2 * w, :], preferred_element_type=F32)
             + jnp.dot(mm_ref[...], w_ref[2 * w:, :], preferred_element_type=F32))
        g1 = mod_ref[0][:, 2 * d:3 * d]
        o_ref[0] = _layer_norm(ALPHA * x_ref[0] + g1 * y, g_ref[...], b_ref[...])

    n_ctx_tiles = n_ctx // TM

    @pl.when(t < n_ctx_tiles)
    def _():
        attend(n_ctx)
        project()

    @pl.when(t >= n_ctx_tiles)
    def _():
        attend(kv_ref.shape[1])
        project()


def _mla_call(mq, mkv, xa, m_gla, m_ret, mod_l, wl, n_ctx, t_off):
    bsz, t_all, d = xa.shape
    nt = t_all // TM - t_off
    n_ctx_tiles = n_ctx // TM
    src = lambda w: pl.BlockSpec((1, TM, w), lambda b, t: (b, t + t_off, 0))
    full = lambda a: pl.BlockSpec(a.shape, lambda b, t: (0,) * a.ndim)
    return pl.pallas_call(
        functools.partial(_mla_kernel, n_ctx=n_ctx, t_off=t_off),
        grid=(bsz, nt),
        in_specs=[src(mq.shape[2]),
                  pl.BlockSpec((1, t_all, mkv.shape[2]), lambda b, t: (b, 0, 0)),
                  src(d), src(V_W), src(V_W),
                  pl.BlockSpec((1, 1, 6 * d),
                               lambda b, t: (2 * b + (t + t_off >= n_ctx_tiles).astype(jnp.int32), 0, 0)),
                  full(wl["w_out"]), full(wl["ln1_g"]), full(wl["ln1_b"])],
        out_specs=pl.BlockSpec((1, TM, d), lambda b, t: (b, t, 0)),
        out_shape=jax.ShapeDtypeStruct((bsz, nt * TM, d), F32),
        scratch_shapes=[pltpu.VMEM((TM, MLA_HEADS * MLA_DV), BF16)],
        compiler_params=_cparams(("arbitrary", "arbitrary")),
        name="mla_attn_out_ln1",
    )(mq, mkv, xa, m_gla, m_ret, mod_l, wl["w_out"], wl["ln1_g"], wl["ln1_b"])


def _ffn_kernel(xp_ref, x_ref, xn_ref, mod_ref, up_ref, cw_ref, cb_ref, down_ref, g_ref, b_ref, o_ref,
                h2_ref, ua_ref, ug_ref, act_ref, *, sub, n_ctx_tiles, seq_starts, seq_ends):
    d = D_MODEL
    n_rows = TM + 2 * HALO
    n_ff = D_FF // FF_CHUNK
    t = pl.program_id(1)
    mod_ctx, mod_lat = mod_ref[0], mod_ref[1]

    def tile_mod(tile):
        if n_ctx_tiles == 0:
            pick = lambda k: mod_lat[:, k * d:(k + 1) * d]
        else:
            is_ctx = tile < n_ctx_tiles
            pick = lambda k: jnp.where(is_ctx, mod_ctx[:, k * d:(k + 1) * d], mod_lat[:, k * d:(k + 1) * d])
        return pick(3), pick(4), pick(5)

    sh_l, sc_l = mod_lat[:, 3 * d:4 * d], mod_lat[:, 4 * d:5 * d]
    h2_ref[0:HALO, :] = (xp_ref[0] * (1.0 + sc_l) + sh_l).astype(BF16)
    h2_ref[HALO + sub * TM:2 * HALO + sub * TM, :] = (xn_ref[0] * (1.0 + sc_l) + sh_l).astype(BF16)
    gates = []
    for s in range(sub):
        sh2, sc2, g2 = tile_mod(t * sub + s)
        gates.append(g2)
        h2_ref[HALO + s * TM:HALO + (s + 1) * TM, :] = (
            x_ref[0, s * TM:(s + 1) * TM, :] * (1.0 + sc2) + sh2).astype(BF16)

    def edge_keep(s):
        tile = t * sub + s
        is_start = functools.reduce(jnp.logical_or, [tile == v for v in seq_starts])
        is_end = functools.reduce(jnp.logical_or, [tile == v for v in seq_ends])
        return jnp.where(is_start, 0.0, 1.0), jnp.where(is_end, 0.0, 1.0)
    keeps = [edge_keep(s) for s in range(sub)]

    def up_stage(s, j, slot):
        keep_prev, keep_next = keeps[s]
        for u_ref, lo in ((ua_ref, j * FF_CHUNK), (ug_ref, D_FF + j * FF_CHUNK)):
            u = jnp.dot(h2_ref[s * TM:s * TM + n_rows, :], up_ref[:, lo:lo + FF_CHUNK],
                        preferred_element_type=F32)
            u_ref[slot, 0:HALO, :] = u[0:HALO] * keep_prev
            u_ref[slot, HALO:HALO + TM, :] = u[HALO:HALO + TM]
            u_ref[slot, HALO + TM:n_rows, :] = u[HALO + TM:n_rows] * keep_next

    def conv(u_ref, slot, lo):
        cw = cw_ref[:, lo:lo + FF_CHUNK]
        u = u_ref[slot]
        prev = pltpu.roll(u * cw[0:1], 1, axis=0)[HALO:HALO + TM]
        nxt = pltpu.roll(u * cw[2:3], n_rows - 1, axis=0)[HALO:HALO + TM]
        return prev + (u[HALO:HALO + TM] * cw[1:2] + cb_ref[:, lo:lo + FF_CHUNK]) + nxt

    def act_stage(s, j, slot):
        a = conv(ua_ref, slot, j * FF_CHUNK)
        gt = conv(ug_ref, slot, D_FF + j * FF_CHUNK)
        act_ref[s % 2, :, j * FF_CHUNK:(j + 1) * FF_CHUNK] = (_silu(a) * gt).astype(BF16)

    items = [(s, j) for s in range(sub) for j in range(n_ff)]
    acc = None
    up_stage(0, 0, 0)
    for k, (s, j) in enumerate(items):
        if k + 1 < len(items):
            up_stage(*items[k + 1], (k + 1) % 2)
        act_stage(s, j, k % 2)
        down = lambda lo, hi: jnp.dot(act_ref[s % 2, :, lo:hi], down_ref[lo:hi, :], preferred_element_type=F32)
        if (j + 1) % DOWN_EVERY == 0 or j == n_ff - 1:
            lo = (j // DOWN_EVERY) * DOWN_EVERY * FF_CHUNK
            part = down(lo, (j + 1) * FF_CHUNK)
            acc = part if lo == 0 else acc + part
        if j == n_ff - 1:
            x1 = x_ref[0, s * TM:(s + 1) * TM, :]
            o_ref[0, s * TM:(s + 1) * TM, :] = _layer_norm(ALPHA * x1 + gates[s] * acc, g_ref[...], b_ref[...])


def _ffn_call(x1, mod_l, wl, n_ctx_tiles, t_off):
    bsz, rows, d = x1.shape
    nt = rows // TM
    sub = next(s for s in FFN_TILES_PER_BLOCK if nt % s == 0)
    br = sub * TM
    hb = br // HALO
    n_hblk = rows // HALO
    n_ctx_local = max(n_ctx_tiles - t_off, 0)
    starts = sorted({max(0 - t_off, 0), n_ctx_local})
    ends = sorted({e for e in (n_ctx_local - 1, nt - 1) if e >= 0})
    full = lambda a: pl.BlockSpec(a.shape, lambda b, t: (0,) * a.ndim, pipeline_mode=pl.Buffered(1))
    weights = (wl["ffn_up"], wl["conv_w"], wl["conv_b"], wl["ffn_down"], wl["ln2_g"], wl["ln2_b"])
    n_rows = TM + 2 * HALO
    return pl.pallas_call(
        functools.partial(_ffn_kernel, sub=sub, n_ctx_tiles=n_ctx_local, seq_starts=tuple(starts),
                          seq_ends=tuple(ends)),
        grid=(bsz, nt // sub),
        in_specs=[pl.BlockSpec((1, HALO, d), lambda b, t: (b, jnp.maximum(t * hb - 1, 0), 0)),
                  pl.BlockSpec((1, br, d), lambda b, t: (b, t, 0)),
                  pl.BlockSpec((1, HALO, d), lambda b, t: (b, jnp.minimum((t + 1) * hb, n_hblk - 1), 0)),
                  pl.BlockSpec((2, 1, 6 * d), lambda b, t: (b, 0, 0))]
                 + [full(w) for w in weights],
        out_specs=pl.BlockSpec((1, br, d), lambda b, t: (b, t, 0)),
        out_shape=jax.ShapeDtypeStruct((bsz, rows, d), F32),
        scratch_shapes=[pltpu.VMEM((br + 2 * HALO, d), BF16),
                        pltpu.VMEM((2, n_rows, FF_CHUNK), F32), pltpu.VMEM((2, n_rows, FF_CHUNK), F32),
                        pltpu.VMEM((2, TM, D_FF), BF16)],
        compiler_params=_cparams(("arbitrary", "arbitrary")),
        name="conv_ffn_ln2",
    )(x1, x1, x1, mod_l, *weights)


def _rope_tables(seq, n_ctx):
    pos = jnp.arange(seq, dtype=F32)
    ret_inv = 1.0 / (ROPE_BASE ** jnp.linspace(0.0, 1.0, DK // 2, dtype=F32))
    ret_ang = pos[:, None] * ret_inv
    rc, rs = jnp.cos(ret_ang), jnp.sin(ret_ang)
    ret_cos = jnp.tile(jnp.concatenate([rc, rc], axis=1), (1, N_HEADS))
    ret_sin = jnp.tile(jnp.concatenate([-rs, rs], axis=1), (1, N_HEADS))

    rows = jnp.repeat(jnp.arange(seq // GRID_W, dtype=F32), GRID_W)
    cols = jnp.tile(jnp.arange(GRID_W, dtype=F32), seq // GRID_W)
    n_ax = MLA_D_ROPE // 4
    ax_inv = ROPE_BASE ** (-jnp.arange(n_ax, dtype=F32) / n_ax)
    ra, ca = rows[:, None] * ax_inv, cols[:, None] * ax_inv
    cos32 = jnp.concatenate([jnp.cos(ra), jnp.cos(ra), jnp.cos(ca), jnp.cos(ca)], axis=1)
    sin32 = jnp.concatenate([-jnp.sin(ra), jnp.sin(ra), -jnp.sin(ca), jnp.sin(ca)], axis=1)
    one, zero = jnp.ones((seq, MLA_D_NOPE), F32), jnp.zeros((seq, 32), F32)
    q_cos = MLA_QSCALE * jnp.concatenate([one, cos32, zero], axis=1)
    q_sin = MLA_QSCALE * jnp.concatenate([0.0 * one, sin32, zero], axis=1)
    misc = jnp.concatenate([cos32, sin32, zero, zero], axis=1)
    lat = jnp.concatenate([ret_cos, ret_sin, q_cos, q_sin, misc], axis=1)

    c1, c0 = jnp.ones((n_ctx, 128), F32), jnp.zeros((n_ctx, 128), F32)
    q_cos_c = MLA_QSCALE * jnp.concatenate([jnp.ones((n_ctx, 96), F32), jnp.zeros((n_ctx, 32), F32)], axis=1)
    misc_c = jnp.concatenate([jnp.ones((n_ctx, 32), F32), jnp.zeros((n_ctx, 96), F32)], axis=1)
    ctx = jnp.concatenate([c1, c0, q_cos_c, c0, misc_c], axis=1)
    return jnp.concatenate([ctx, lat], axis=0)


def _swap_idx(n, half):
    j = np.arange(n)
    return (j // (2 * half)) * (2 * half) + (j % (2 * half) + half) % (2 * half)


def _layer_weights(i, w_in, gla_gate_w, gla_gate_b, gla_norm_g, ret_decay, mla_q_norm_g, mla_kv_norm_g,
                   mla_w_uq, mla_w_uk, mla_w_uv, w_out, ln1_g, ln1_b, ffn_up, ffn_conv_w, ffn_conv_b,
                   ffn_down, ln2_g, ln2_b):
    d = D_MODEL
    o = IN_OFFS
    wi = w_in[i]
    sec = lambda k: wi[:, o[k]:o[k + 1]]
    swap_ret = _swap_idx(QK_W, DK // 2)
    swap_kr = _swap_idx(MLA_D_ROPE, MLA_D_ROPE // 4)
    kr = sec(11)
    misc = jnp.concatenate([kr, kr[:, swap_kr], sec(3), jnp.zeros((d, 32), F32)], axis=1)
    w_in_p = jnp.concatenate([sec(0), sec(1), sec(2), sec(4), sec(5), sec(6), sec(7), sec(8), sec(9),
                              sec(10), misc, sec(5)[:, swap_ret], sec(6)[:, swap_ret]], axis=1).astype(BF16)

    wg = jnp.zeros((128, 2 * QK_W), F32)
    wg = wg.at[64:64 + GATE_RANK, 0:QK_W].set(gla_gate_w[i, 0])
    wg = wg.at[64 + GATE_RANK:64 + 2 * GATE_RANK, QK_W:].set(gla_gate_w[i, 1])
    bg = jnp.concatenate([gla_gate_b[i, 0], gla_gate_b[i, 1]])[None, :]

    uq = mla_w_uq[i].reshape(MLA_Q_RANK, MLA_HEADS, MLA_D_NOPE + MLA_D_ROPE)
    z32 = jnp.zeros((MLA_Q_RANK, MLA_HEADS, 32), F32)
    z64 = jnp.zeros((MLA_Q_RANK, MLA_HEADS, 64), F32)
    uq_main = jnp.concatenate([uq, z32], axis=2).reshape(MLA_Q_RANK, -1)
    uq_sw = jnp.concatenate([z64, uq[:, :, MLA_D_NOPE:][:, :, swap_kr], z32], axis=2).reshape(MLA_Q_RANK, -1)
    w_uq = jnp.concatenate([uq_main, uq_sw], axis=1).astype(BF16)

    uk = mla_w_uk[i].reshape(MLA_KV_RANK, MLA_HEADS, MLA_D_NOPE)
    uk_pad = jnp.concatenate([uk, jnp.zeros((MLA_KV_RANK, MLA_HEADS, 64), F32)], axis=2).reshape(MLA_KV_RANK, -1)
    uv = mla_w_uv[i].reshape(MLA_KV_RANK, MLA_HEADS, MLA_DV)
    even = (jnp.arange(MLA_HEADS) % 2 == 0)[None, :, None]
    uv_a = jnp.where(even, uv, 0.0).reshape(MLA_KV_RANK, -1)
    uv_b = jnp.where(even, 0.0, uv).reshape(MLA_KV_RANK, -1)
    place = jnp.zeros((128, MLA_HEADS, MLA_HEAD_PAD), F32)
    eye = jnp.eye(MLA_D_ROPE, dtype=F32)
    place = place.at[0:MLA_D_ROPE, :, MLA_D_NOPE:MLA_D_NOPE + MLA_D_ROPE].set(
        jnp.broadcast_to(eye[:, None, :], (MLA_D_ROPE, MLA_HEADS, MLA_D_ROPE)))
    place = place.reshape(128, -1)
    w_kv = jnp.concatenate([
        jnp.concatenate([uk_pad, uv_a, uv_b], axis=1),
        jnp.concatenate([place, jnp.zeros((128, 2 * MLA_HEADS * MLA_DV), F32)], axis=1)], axis=0).astype(BF16)

    row = lambda a: a[None, :]
    return {
        "w_in": w_in_p, "w_gate": wg.astype(BF16), "b_gate": bg,
        "qn_g": row(mla_q_norm_g[i]), "kvn_g": row(mla_kv_norm_g[i]), "w_uq": w_uq, "w_kv": w_kv,
        "gla_ng": row(jnp.tile(gla_norm_g[i], N_HEADS)),
        "dec_head": jnp.broadcast_to(ret_decay[i][:, :, None, None], (2, N_HEADS, 1, SCAN_CHUNK)),
        "dec_lanes": jnp.repeat(ret_decay[i], DK, axis=1)[:, None, :],
        "w_out": w_out[i].astype(BF16), "ln1_g": row(ln1_g[i]), "ln1_b": row(ln1_b[i]),
        "ffn_up": ffn_up[i].astype(BF16), "conv_w": ffn_conv_w[i], "conv_b": row(ffn_conv_b[i]),
        "ffn_down": ffn_down[i].astype(BF16), "ln2_g": row(ln2_g[i]), "ln2_b": row(ln2_b[i]),
    }


def kernel(x, c, ctx, c_ctx, ada_w, ada_b, w_in, gla_gate_w, gla_gate_b, gla_norm_g, ret_decay, mla_q_norm_g, mla_kv_norm_g, mla_w_uq, mla_w_uk, mla_w_uv, w_out, ln1_g, ln1_b, ffn_up, ffn_conv_w, ffn_conv_b, ffn_down, ln2_g, ln2_b):
    bsz, seq, d = x.shape
    n_ctx = ctx.shape[1]
    assert d == D_MODEL and seq % TM == 0 and n_ctx % TM == 0 and seq % GRID_W == 0
    n_ctx_tiles = n_ctx // TM

    rows = -(-(bsz + 1) // 8) * 8
    c_all = jnp.concatenate([c, c_ctx[None, :], jnp.zeros((rows - bsz - 1, d), F32)], axis=0)
    mod = _ada_call(c_all, ada_w, ada_b)
    mod_ctx = jnp.broadcast_to(mod[:, bsz:bsz + 1], (DEPTH, bsz, 6 * d))
    mod_sel = jnp.stack([mod_ctx, mod[:, :bsz]], axis=2).reshape(DEPTH, 2 * bsz, 1, 6 * d)

    tab = _rope_tables(seq, n_ctx)
    xa = jnp.concatenate([ctx, x], axis=1)
    dmat_scratch = [pltpu.VMEM((2, N_HEADS, SCAN_CHUNK, SCAN_CHUNK), F32)]

    for i in range(DEPTH):
        wl = _layer_weights(i, w_in, gla_gate_w, gla_gate_b, gla_norm_g, ret_decay, mla_q_norm_g,
                            mla_kv_norm_g, mla_w_uq, mla_w_uk, mla_w_uv, w_out, ln1_g, ln1_b, ffn_up,
                            ffn_conv_w, ffn_conv_b, ffn_down, ln2_g, ln2_b)
        mod_l = mod_sel[i]
        need_ctx = i < DEPTH - 1
        t_off = 0 if need_ctx else n_ctx_tiles
        g1, gv, gg, r1, rv, rg, mq, mkv = _proj_call(xa, mod_l, tab, wl, n_ctx_tiles)
        m_gla = _scan_call(_gla_kernel, "gla_scan", (g1, gv, gg), (wl["gla_ng"],), [], n_ctx)
        m_ret = _scan_call(_ret_kernel, "ret_scan", (r1, rv, rg), (wl["dec_head"], wl["dec_lanes"]),
                           dmat_scratch, n_ctx)
        x1 = _mla_call(mq, mkv, xa, m_gla, m_ret, mod_l, wl, n_ctx, t_off)
        xa = _ffn_call(x1, mod_l, wl, n_ctx_tiles, t_off)
    return xa
```

```python
import functools

import numpy as np
import jax
import jax.numpy as jnp
from jax import lax
from jax.experimental import pallas as pl
from jax.experimental.pallas import tpu as pltpu

F32 = jnp.float32
BF16 = jnp.bfloat16

D_MODEL = 1024
DEPTH = 4
GRID_W = 64
N_HEADS = 4
DK = 32
DV = 64
QK_W = N_HEADS * DK
V_W = N_HEADS * DV
GATE_RANK = 16
GLA_TAU = 16.0
MLA_HEADS = 8
MLA_D_NOPE = 64
MLA_D_ROPE = 32
MLA_DV = 64
MLA_Q_RANK = 256
MLA_KV_RANK = 128
MLA_HEAD_PAD = 128
MLA_SCALE = (MLA_D_NOPE + MLA_D_ROPE) ** -0.5
MLA_QSCALE = MLA_SCALE * float(np.log2(np.e))
D_FF = 2816
ROPE_BASE = 10000.0
EPS = 1e-6
ALPHA = (2 * DEPTH) ** 0.25
IN_SIZES = (QK_W, QK_W, V_W, 2 * GATE_RANK, V_W, QK_W, QK_W, V_W, V_W,
            MLA_Q_RANK, MLA_KV_RANK, MLA_D_ROPE)
IN_OFFS = tuple(int(o) for o in np.concatenate([[0], np.cumsum(IN_SIZES)]))

P_GQ, P_GK, P_GV, P_GG = 0, 128, 256, 512
P_RQ, P_RK, P_RV, P_RG = 768, 896, 1024, 1280
P_CQ, P_CKV, P_MISC = 1536, 1792, 1920
P_RQS, P_RKS = 2048, 2176
P_W = 2304

TM = 256
SCAN_CHUNK = 256
GLA_MAX_CHUNK_LOG_DECAY = 64.0
HALO = 8
FF_CHUNK = 256
FFN_TILES_PER_BLOCK = (3, 4, 2, 1)
DOWN_EVERY = 6
V7X_VMEM_BYTES = 64 * 1024 * 1024
VMEM_RESERVE_BYTES = 4 * 1024 * 1024
VMEM_VALUES_BYTES = 16 * 1024 * 1024


def _nbytes(shape, dtype):
    return int(np.prod(shape)) * jnp.dtype(dtype).itemsize


def _pallas(kernel, *, name, grid, in_specs, operands, out_specs, out_shape, scratch_shapes=()):
    out_list = out_shape if isinstance(out_shape, (list, tuple)) else [out_shape]
    spec_list = out_specs if isinstance(out_specs, (list, tuple)) else [out_specs]

    def block_bytes(spec, dtype):
        buffers = 2 if spec.pipeline_mode is None else spec.pipeline_mode.buffer_count
        return buffers * _nbytes(spec.block_shape, dtype)

    resident = (sum(block_bytes(s, a.dtype) for s, a in zip(in_specs, operands))
                + sum(block_bytes(s, o.dtype) for s, o in zip(spec_list, out_list))
                + sum(_nbytes(s.shape, s.dtype) for s in scratch_shapes))
    limit = min(resident + VMEM_VALUES_BYTES, V7X_VMEM_BYTES - VMEM_RESERVE_BYTES)
    return pl.pallas_call(
        kernel, grid=grid, in_specs=in_specs, out_specs=out_specs, out_shape=out_shape,
        scratch_shapes=list(scratch_shapes),
        compiler_params=pltpu.CompilerParams(dimension_semantics=("arbitrary",) * len(grid),
                                             vmem_limit_bytes=limit),
        name=name,
    )(*operands)


def _iota(shape, dim):
    return lax.broadcasted_iota(jnp.int32, shape, dim)


def _split3(x):
    hi = x.astype(BF16)
    r1 = x - hi.astype(F32)
    mid = r1.astype(BF16)
    lo = (r1 - mid.astype(F32)).astype(BF16)
    return hi, mid, lo


def _dot_exact_rhs01(x, m01):
    hi, mid, lo = _split3(x)
    d = lambda t: jnp.dot(t, m01, preferred_element_type=F32)
    return d(hi) + d(mid) + d(lo)


def _log_sigmoid(z):
    return jnp.minimum(z, 0.0) - jnp.log1p(jnp.exp(-jnp.abs(z)))


def _silu(x):
    return x * jax.nn.sigmoid(x)


def _layer_norm(x, g, b):
    mu = jnp.mean(x, axis=-1, keepdims=True)
    xc = x - mu
    var = jnp.mean(xc * xc, axis=-1, keepdims=True)
    return xc * lax.rsqrt(var + EPS) * g + b


def _rms_norm(x, g):
    return x * lax.rsqrt(jnp.mean(x * x, axis=-1, keepdims=True) + EPS) * g


def _dot_nt(a, b):
    return lax.dot_general(a, b, (((1,), (1,)), ((), ())), preferred_element_type=F32)


def _dot_tn(a, b):
    return lax.dot_general(a, b, (((0,), (0,)), ((), ())), preferred_element_type=F32)


def _ada_kernel(c_ref, w_ref, b_ref, o_ref):
    a = _silu(c_ref[...]).astype(BF16)
    o_ref[0] = jnp.dot(a, w_ref[0].astype(BF16), preferred_element_type=F32) + b_ref[0]


def _ada_call(c_all, ada_w, ada_b):
    depth, d, e = ada_w.shape
    rows = c_all.shape[0]
    bn = 1024
    return _pallas(
        _ada_kernel,
        name="ada_mod",
        grid=(depth, e // bn),
        in_specs=[pl.BlockSpec((rows, d), lambda l, n: (0, 0)),
                  pl.BlockSpec((1, d, bn), lambda l, n: (l, 0, n)),
                  pl.BlockSpec((1, 1, bn), lambda l, n: (l, 0, n))],
        operands=(c_all, ada_w, ada_b.reshape(depth, 1, e)),
        out_specs=pl.BlockSpec((1, rows, bn), lambda l, n: (l, 0, n)),
        out_shape=jax.ShapeDtypeStruct((depth, rows, e), F32),
    )


def _proj_kernel(x_ref, mod_ref, tab_ref, w_in_ref, w_gate_ref, b_gate_ref, qn_g_ref, kvn_g_ref,
                 w_uq_ref, w_kv_ref,
                 g1_ref, gv_ref, gg_ref, r1_ref, rv_ref, rg_ref, mq_ref, mkv_ref):
    d = D_MODEL
    x = x_ref[0]
    mod = mod_ref[0]
    h = (x * (1.0 + mod[:, d:2 * d]) + mod[:, 0:d]).astype(BF16)
    p = jnp.dot(h, w_in_ref[...], preferred_element_type=F32)

    ret_cos = tab_ref[:, 0:128]
    ret_sin = tab_ref[:, 128:256]
    q_cos = tab_ref[:, 256:384]
    q_sin = tab_ref[:, 384:512]
    misc_tab = tab_ref[:, 512:640]

    misc = p[:, P_MISC:P_MISC + 128]
    z = jnp.dot(misc.astype(BF16), w_gate_ref[...], preferred_element_type=F32) + b_gate_ref[...]
    log_a = _log_sigmoid(z) / GLA_TAU
    g1_ref[0, :, 0:128] = p[:, P_GQ:P_GQ + 128] * (DK ** -0.5)
    g1_ref[0, :, 128:256] = p[:, P_GK:P_GK + 128]
    g1_ref[0, :, 256:512] = log_a
    gv_ref[0] = p[:, P_GV:P_GV + V_W].astype(BF16)
    gg_ref[0] = p[:, P_GG:P_GG + V_W]

    r1_ref[0, :, 0:128] = p[:, P_RQ:P_RQ + 128] * ret_cos + p[:, P_RQS:P_RQS + 128] * ret_sin
    ks = DK ** -0.5
    r1_ref[0, :, 128:256] = ((p[:, P_RK:P_RK + 128] * ks) * ret_cos
                             + (p[:, P_RKS:P_RKS + 128] * ks) * ret_sin)
    rv_ref[0] = p[:, P_RV:P_RV + V_W].astype(BF16)
    rg_ref[0] = p[:, P_RG:P_RG + V_W]

    cq = _rms_norm(p[:, P_CQ:P_CQ + MLA_Q_RANK], qn_g_ref[...]).astype(BF16)
    qq = jnp.dot(cq, w_uq_ref[...], preferred_element_type=F32)
    qw = MLA_HEADS * MLA_HEAD_PAD
    for hh in range(MLA_HEADS):
        lo = hh * MLA_HEAD_PAD
        mq_ref[0, :, lo:lo + MLA_HEAD_PAD] = (
            qq[:, lo:lo + MLA_HEAD_PAD] * q_cos + qq[:, qw + lo:qw + lo + MLA_HEAD_PAD] * q_sin
        ).astype(BF16)

    ckv = _rms_norm(p[:, P_CKV:P_CKV + MLA_KV_RANK], kvn_g_ref[...])
    prod = misc * misc_tab
    kr = prod + pltpu.roll(prod, 128 - MLA_D_ROPE, axis=1)
    lhs = jnp.concatenate([ckv, kr], axis=1).astype(BF16)
    mkv_ref[0] = jnp.dot(lhs, w_kv_ref[...], preferred_element_type=F32).astype(BF16)


def _proj_call(xa, mod_l, tab, wl, n_ctx_tiles):
    bsz, t_all, d = xa.shape
    nt = t_all // TM
    tile = lambda w: pl.BlockSpec((1, TM, w), lambda b, t: (b, t, 0))
    full = lambda a: pl.BlockSpec(a.shape, lambda b, t: (0,) * a.ndim)
    sds = lambda w, dt: jax.ShapeDtypeStruct((bsz, t_all, w), dt)
    weights = (wl["w_in"], wl["w_gate"], wl["b_gate"], wl["qn_g"], wl["kvn_g"], wl["w_uq"], wl["w_kv"])
    return _pallas(
        _proj_kernel,
        name="proj",
        grid=(bsz, nt),
        in_specs=[tile(d),
                  pl.BlockSpec((1, 1, 6 * d), lambda b, t: (2 * b + (t >= n_ctx_tiles).astype(jnp.int32), 0, 0)),
                  pl.BlockSpec((TM, tab.shape[1]), lambda b, t: (t, 0))]
                 + [full(w) for w in weights],
        operands=(xa, mod_l, tab, *weights),
        out_specs=[tile(512), tile(V_W), tile(V_W), tile(256), tile(V_W), tile(V_W),
                   tile(MLA_HEADS * MLA_HEAD_PAD), tile(2048)],
        out_shape=[sds(512, F32), sds(V_W, BF16), sds(V_W, F32), sds(256, F32), sds(V_W, BF16),
                   sds(V_W, F32), sds(MLA_HEADS * MLA_HEAD_PAD, BF16), sds(2048, BF16)],
    )


def _key_lane_masks():
    lane_head = _iota((1, QK_W), 1) // DK
    return [jnp.where(lane_head == h, 1.0, 0.0).astype(BF16) for h in range(N_HEADS)]


def _state_mask():
    return jnp.where((_iota((V_W, QK_W), 0) // DV) == (_iota((V_W, QK_W), 1) // DK), 1.0, 0.0)


def _fill_head_values(v_ref, v4_ref, n_rows):
    lane_head = _iota((TM, V_W), 1) // DV

    def body(i, carry):
        r0 = pl.multiple_of(i * TM, TM)
        v = v_ref[0, pl.ds(r0, TM), :].astype(F32)
        for h in range(N_HEADS):
            v4_ref[h, pl.ds(r0, TM), :] = jnp.where(lane_head == h, v, 0.0).astype(BF16)
        return carry
    lax.fori_loop(0, n_rows // TM, body, 0)


def _chunk_step(qe, ke, weights, qdec, kdec, sdec, s, v_ref, v4_ref, r0, key_masks, state_mask):
    c = SCAN_CHUNK
    o = None
    for h in range(N_HEADS):
        att = _dot_nt(qe * key_masks[h], ke) * weights[h]
        part = jnp.dot(att.astype(BF16), v4_ref[h, pl.ds(r0, c), :], preferred_element_type=F32)
        o = part if o is None else o + part
    o = o + _dot_nt(qdec, s.astype(BF16))
    upd = _dot_tn(v_ref[0, pl.ds(r0, c), :], kdec)
    return o, s * sdec + upd * state_mask


def _run_scan(chunk_fn, sf_ref, sb_ref, of_ref, ob_ref, n_ctx_chunks, n_chunks):
    s_f = jnp.zeros(sf_ref.shape, F32)
    s_b = jnp.zeros(sb_ref.shape, F32)
    for i in range(n_chunks):
        cb = n_ctx_chunks - 1 - i if i < n_ctx_chunks else n_chunks + n_ctx_chunks - 1 - i
        o_f, s_f = chunk_fn(i * SCAN_CHUNK, True, s_f)
        o_b, s_b = chunk_fn(cb * SCAN_CHUNK, False, s_b)
        of_ref[i * SCAN_CHUNK:(i + 1) * SCAN_CHUNK, :] = o_f
        ob_ref[cb * SCAN_CHUNK:(cb + 1) * SCAN_CHUNK, :] = o_b


def _group_ones():
    same = (_iota((V_W, V_W), 0) // DV) == (_iota((V_W, V_W), 1) // DV)
    return jnp.where(same, 1.0, 0.0).astype(BF16)


def _gla_kernel(g1_ref, gv_ref, gg_ref, ng_ref, out_ref, of_ref, ob_ref, sf_ref, sb_ref, v4_ref, *,
                n_ctx_chunks, n_chunks):
    c = SCAN_CHUNK
    key_masks = _key_lane_masks()
    state_mask = _state_mask()
    ti, si = _iota((c, c), 0), _iota((c, c), 1)
    causal = {True: jnp.where(si <= ti, 1.0, 0.0), False: jnp.where(si >= ti, 1.0, 0.0)}
    tri = {fwd: m.astype(BF16) for fwd, m in causal.items()}
    _fill_head_values(gv_ref, v4_ref, n_chunks * c)

    def chunk_fn(r0, fwd, s):
        q = g1_ref[0, pl.ds(r0, c), 0:128]
        k = g1_ref[0, pl.ds(r0, c), 128:256]
        a = g1_ref[0, pl.ds(r0, c), 256:384] if fwd else g1_ref[0, pl.ds(r0, c), 384:512]
        a_hi = a.astype(BF16)
        a_lo = (a - a_hi.astype(F32)).astype(BF16)
        b = (jnp.dot(tri[fwd], a_hi, preferred_element_type=F32)
             + jnp.dot(tri[fwd], a_lo, preferred_element_type=F32))
        b_mid = b[c // 2:c // 2 + 1, :]
        b_end = b[c - 1:c, :] if fwd else b[0:1, :]
        qe = (q * jnp.exp(b - b_mid)).astype(BF16)
        ke = (k * jnp.exp(b_mid - b)).astype(BF16)
        qdec = (q * jnp.exp(b)).astype(BF16)
        kdec = (k * jnp.exp(b_end - b)).astype(BF16)
        return _chunk_step(qe, ke, [causal[fwd]] * N_HEADS, qdec, kdec, jnp.exp(b_end), s,
                           gv_ref, v4_ref, r0, key_masks, state_mask)

    def token_scan():
        of_ref[...] = jnp.zeros_like(of_ref)
        ob_ref[...] = jnp.zeros_like(ob_ref)
        sf_ref[...] = jnp.zeros_like(sf_ref)
        sb_ref[...] = jnp.zeros_like(sb_ref)
        row = _iota((c, 1), 0)

        def token(r0, t, fwd):
            s_ref, o_ref = (sf_ref, of_ref) if fwd else (sb_ref, ob_ref)
            pick = row == t
            q = jnp.where(pick, g1_ref[0, pl.ds(r0, c), 0:128], 0.0).astype(BF16)
            k = jnp.where(pick, g1_ref[0, pl.ds(r0, c), 128:256], 0.0).astype(BF16)
            a = g1_ref[0, pl.ds(r0, c), 256:384] if fwd else g1_ref[0, pl.ds(r0, c), 384:512]
            a_t = jnp.sum(jnp.where(pick, a, 0.0), axis=0, keepdims=True)
            s = s_ref[...] * jnp.exp(a_t) + _dot_tn(gv_ref[0, pl.ds(r0, c), :], k) * state_mask
            s_ref[...] = s
            o_ref[pl.ds(r0, c), :] += _dot_nt(q, s.astype(BF16))

        def chunk_body(i, carry):
            rf = pl.multiple_of(i * c, c)
            cb = jnp.where(i < n_ctx_chunks, n_ctx_chunks - 1 - i, n_chunks + n_ctx_chunks - 1 - i)
            rb = pl.multiple_of(cb * c, c)

            def token_body(j, carry2):
                token(rf, j, True)
                token(rb, c - 1 - j, False)
                return carry2
            return lax.fori_loop(0, c, token_body, carry)
        lax.fori_loop(0, n_chunks, chunk_body, 0)

    def chunk_total(i, m):
        r0 = pl.multiple_of(i * c, c)
        return jnp.maximum(m, -jnp.sum(g1_ref[0, pl.ds(r0, c), 256:512], axis=0, keepdims=True))
    worst = jnp.max(lax.fori_loop(0, n_chunks, chunk_total, jnp.zeros((1, 2 * QK_W), F32)))
    chunked_ok = worst < GLA_MAX_CHUNK_LOG_DECAY

    @pl.when(chunked_ok)
    def _():
        _run_scan(chunk_fn, sf_ref, sb_ref, of_ref, ob_ref, n_ctx_chunks, n_chunks)

    @pl.when(jnp.logical_not(chunked_ok))
    def _():
        token_scan()

    ones_g = _group_ones()
    ng = ng_ref[...]

    for i in range((n_chunks * c) // TM):
        rows = slice(i * TM, (i + 1) * TM)
        o = of_ref[rows, :] + ob_ref[rows, :]
        ms = _dot_exact_rhs01(o * o, ones_g) * (1.0 / DV)
        y = o * lax.rsqrt(ms + EPS) * ng * _silu(gg_ref[0, rows, :])
        out_ref[0, rows, :] = y.astype(BF16)


def _ret_kernel(r1_ref, rv_ref, rg_ref, dec_head_ref, dec_lanes_ref, out_ref, of_ref, ob_ref, sf_ref,
                sb_ref, v4_ref, dm_ref, *, n_ctx_chunks, n_chunks):
    c = SCAN_CHUNK
    key_masks = _key_lane_masks()
    state_mask = _state_mask()
    ti, si = _iota((c, c), 0), _iota((c, c), 1)
    tl = _iota((c, QK_W), 0).astype(F32)
    _fill_head_values(rv_ref, v4_ref, n_chunks * c)

    consts = []
    for di in range(2):
        fwd = di == 0
        rel = (ti - si) if fwd else (si - ti)
        relf = jnp.maximum(rel, 0).astype(F32)
        for h in range(N_HEADS):
            lg = _log_sigmoid(dec_head_ref[di, h])
            dm_ref[di, h] = jnp.where(rel >= 0, jnp.exp(relf * lg), 0.0)
        lg_lanes = _log_sigmoid(dec_lanes_ref[di])
        n = tl if fwd else (c - 1.0) - tl
        consts.append((jnp.exp((n + 1.0) * lg_lanes), jnp.exp((c - 1.0 - n) * lg_lanes),
                       jnp.exp(float(c) * lg_lanes)))

    def chunk_fn(r0, fwd, s):
        di = 0 if fwd else 1
        q_dec, k_dec, s_dec = consts[di]
        q = r1_ref[0, pl.ds(r0, c), 0:128]
        k = r1_ref[0, pl.ds(r0, c), 128:256]
        return _chunk_step(q.astype(BF16), k.astype(BF16), [dm_ref[di, h] for h in range(N_HEADS)],
                           (q * q_dec).astype(BF16), (k * k_dec).astype(BF16), s_dec, s,
                           rv_ref, v4_ref, r0, key_masks, state_mask)

    _run_scan(chunk_fn, sf_ref, sb_ref, of_ref, ob_ref, n_ctx_chunks, n_chunks)

    ones_g = _group_ones()

    for i in range((n_chunks * c) // TM):
        rows = slice(i * TM, (i + 1) * TM)
        o = of_ref[rows, :] + ob_ref[rows, :]
        mu = _dot_exact_rhs01(o, ones_g) * (1.0 / DV)
        oc = o - mu
        var = _dot_exact_rhs01(oc * oc, ones_g) * (1.0 / DV)
        y = oc * lax.rsqrt(var + EPS) * _silu(rg_ref[0, rows, :])
        out_ref[0, rows, :] = y.astype(BF16)


def _scan_call(kernel, name, seq_inputs, small_inputs, extra_scratch, n_ctx):
    bsz, t_all, _ = seq_inputs[0].shape
    assert t_all % SCAN_CHUNK == 0 and n_ctx % SCAN_CHUNK == 0
    per_b = lambda a: pl.BlockSpec((1,) + a.shape[1:], lambda b: (b, 0, 0))
    full = lambda a: pl.BlockSpec(a.shape, lambda b: (0,) * a.ndim)
    state = pltpu.VMEM((V_W, QK_W), F32)
    o_dir = pltpu.VMEM((t_all, V_W), F32)
    return _pallas(
        functools.partial(kernel, n_ctx_chunks=n_ctx // SCAN_CHUNK, n_chunks=t_all // SCAN_CHUNK),
        name=name,
        grid=(bsz,),
        in_specs=[per_b(a) for a in seq_inputs] + [full(a) for a in small_inputs],
        operands=(*seq_inputs, *small_inputs),
        out_specs=pl.BlockSpec((1, t_all, V_W), lambda b: (b, 0, 0)),
        out_shape=jax.ShapeDtypeStruct((bsz, t_all, V_W), BF16),
        scratch_shapes=[o_dir, o_dir, state, state, pltpu.VMEM((N_HEADS, t_all, V_W), BF16)] + extra_scratch,
    )


def _mla_kernel(q_ref, kv_ref, x_ref, mg_ref, mr_ref, mod_ref, w_ref, g_ref, b_ref, o_ref, mm_ref, *,
                n_ctx, t_off):
    d = D_MODEL
    t = pl.program_id(1) + t_off
    kw = MLA_HEADS * MLA_HEAD_PAD
    vw = MLA_HEADS * MLA_DV
    first_half = _iota((1, 128), 1) < MLA_DV

    def attend(n_keys):
        for pair in range(MLA_HEADS // 2):
            pv, rinv = [], []
            for hh in range(2):
                lo = (2 * pair + hh) * MLA_HEAD_PAD
                s = _dot_nt(q_ref[0, :, lo:lo + MLA_HEAD_PAD], kv_ref[0, 0:n_keys, lo:lo + MLA_HEAD_PAD])
                e = jnp.exp2(s - jnp.max(s, axis=-1, keepdims=True))
                rinv.append(1.0 / jnp.sum(e, axis=-1, keepdims=True))
                vlo = kw + hh * vw + pair * 128
                pv.append(jnp.dot(e.astype(BF16), kv_ref[0, 0:n_keys, vlo:vlo + 128],
                                  preferred_element_type=F32))
            scale = jnp.where(first_half, rinv[0], rinv[1])
            mm_ref[:, pair * 128:(pair + 1) * 128] = ((pv[0] + pv[1]) * scale).astype(BF16)

    def project():
        w = V_W
        y = (jnp.dot(mg_ref[0], w_ref[0:w, :], preferred_element_type=F32)
             + jnp.dot(mr_ref[0], w_ref[w:2 * w, :], preferred_element_type=F32))
        y = y + jnp.dot(mm_ref[...], w_ref[2 * w:, :], preferred_element_type=F32)
        g1 = mod_ref[0][:, 2 * d:3 * d]
        o_ref[0] = _layer_norm(ALPHA * x_ref[0] + g1 * y, g_ref[...], b_ref[...])

    n_ctx_tiles = n_ctx // TM

    @pl.when(t < n_ctx_tiles)
    def _():
        attend(n_ctx)
        project()

    @pl.when(t >= n_ctx_tiles)
    def _():
        attend(kv_ref.shape[1])
        project()


def _mla_call(mq, mkv, xa, m_gla, m_ret, mod_l, wl, n_ctx, t_off):
    bsz, t_all, d = xa.shape
    nt = t_all // TM - t_off
    n_ctx_tiles = n_ctx // TM
    src = lambda w: pl.BlockSpec((1, TM, w), lambda b, t: (b, t + t_off, 0))
    full = lambda a: pl.BlockSpec(a.shape, lambda b, t: (0,) * a.ndim)
    return _pallas(
        functools.partial(_mla_kernel, n_ctx=n_ctx, t_off=t_off),
        name="mla_attn_out_ln1",
        grid=(bsz, nt),
        operands=(mq, mkv, xa, m_gla, m_ret, mod_l, wl["w_out"], wl["ln1_g"], wl["ln1_b"]),
        in_specs=[src(mq.shape[2]),
                  pl.BlockSpec((1, t_all, mkv.shape[2]), lambda b, t: (b, 0, 0)),
                  src(d), src(V_W), src(V_W),
                  pl.BlockSpec((1, 1, 6 * d),
                               lambda b, t: (2 * b + (t + t_off >= n_ctx_tiles).astype(jnp.int32), 0, 0)),
                  full(wl["w_out"]), full(wl["ln1_g"]), full(wl["ln1_b"])],
        out_specs=pl.BlockSpec((1, TM, d), lambda b, t: (b, t, 0)),
        out_shape=jax.ShapeDtypeStruct((bsz, nt * TM, d), F32),
        scratch_shapes=[pltpu.VMEM((TM, MLA_HEADS * MLA_DV), BF16)],
    )


def _ffn_kernel(xp_ref, x_ref, xn_ref, mod_ref, up_ref, cw_ref, cb_ref, down_ref, g_ref, b_ref, o_ref,
                h2_ref, ua_ref, ug_ref, act_ref, *, sub, n_ctx_tiles, seq_starts, seq_ends):
    d = D_MODEL
    n_rows = TM + 2 * HALO
    n_ff = D_FF // FF_CHUNK
    t = pl.program_id(1)
    mod_ctx, mod_lat = mod_ref[0], mod_ref[1]

    def tile_mod(tile):
        if n_ctx_tiles == 0:
            pick = lambda k: mod_lat[:, k * d:(k + 1) * d]
        else:
            is_ctx = tile < n_ctx_tiles
            pick = lambda k: jnp.where(is_ctx, mod_ctx[:, k * d:(k + 1) * d], mod_lat[:, k * d:(k + 1) * d])
        return pick(3), pick(4), pick(5)

    sh_l, sc_l = mod_lat[:, 3 * d:4 * d], mod_lat[:, 4 * d:5 * d]
    h2_ref[0:HALO, :] = (xp_ref[0] * (1.0 + sc_l) + sh_l).astype(BF16)
    h2_ref[HALO + sub * TM:2 * HALO + sub * TM, :] = (xn_ref[0] * (1.0 + sc_l) + sh_l).astype(BF16)
    gates = []
    for s in range(sub):
        sh2, sc2, g2 = tile_mod(t * sub + s)
        gates.append(g2)
        h2_ref[HALO + s * TM:HALO + (s + 1) * TM, :] = (
            x_ref[0, s * TM:(s + 1) * TM, :] * (1.0 + sc2) + sh2).astype(BF16)

    def edge_keep(s):
        tile = t * sub + s
        is_start = functools.reduce(jnp.logical_or, [tile == v for v in seq_starts])
        is_end = functools.reduce(jnp.logical_or, [tile == v for v in seq_ends])
        return jnp.where(is_start, 0.0, 1.0), jnp.where(is_end, 0.0, 1.0)
    keeps = [edge_keep(s) for s in range(sub)]

    def up_stage(s, j, slot):
        keep_prev, keep_next = keeps[s]
        for u_ref, lo in ((ua_ref, j * FF_CHUNK), (ug_ref, D_FF + j * FF_CHUNK)):
            u = jnp.dot(h2_ref[s * TM:s * TM + n_rows, :], up_ref[:, lo:lo + FF_CHUNK],
                        preferred_element_type=F32)
            u_ref[slot, 0:HALO, :] = u[0:HALO] * keep_prev
            u_ref[slot, HALO:HALO + TM, :] = u[HALO:HALO + TM]
            u_ref[slot, HALO + TM:n_rows, :] = u[HALO + TM:n_rows] * keep_next

    def conv(u_ref, slot, lo):
        cw = cw_ref[:, lo:lo + FF_CHUNK]
        u = u_ref[slot]
        prev = pltpu.roll(u * cw[0:1], 1, axis=0)[HALO:HALO + TM]
        nxt = pltpu.roll(u * cw[2:3], n_rows - 1, axis=0)[HALO:HALO + TM]
        return prev + (u[HALO:HALO + TM] * cw[1:2] + cb_ref[:, lo:lo + FF_CHUNK]) + nxt

    def act_stage(s, j, slot):
        a = conv(ua_ref, slot, j * FF_CHUNK)
        gt = conv(ug_ref, slot, D_FF + j * FF_CHUNK)
        act_ref[s % 2, :, j * FF_CHUNK:(j + 1) * FF_CHUNK] = (_silu(a) * gt).astype(BF16)

    items = [(s, j) for s in range(sub) for j in range(n_ff)]
    acc = None
    up_stage(0, 0, 0)
    for k, (s, j) in enumerate(items):
        if k + 1 < len(items):
            up_stage(*items[k + 1], (k + 1) % 2)
        act_stage(s, j, k % 2)
        down = lambda lo, hi: jnp.dot(act_ref[s % 2, :, lo:hi], down_ref[lo:hi, :], preferred_element_type=F32)
        if (j + 1) % DOWN_EVERY == 0 or j == n_ff - 1:
            lo = (j // DOWN_EVERY) * DOWN_EVERY * FF_CHUNK
            part = down(lo, (j + 1) * FF_CHUNK)
            acc = part if lo == 0 else acc + part
        if j == n_ff - 1:
            x1 = x_ref[0, s * TM:(s + 1) * TM, :]
            o_ref[0, s * TM:(s + 1) * TM, :] = _layer_norm(ALPHA * x1 + gates[s] * acc, g_ref[...], b_ref[...])


def _ffn_call(x1, mod_l, wl, n_ctx_tiles, t_off):
    bsz, rows, d = x1.shape
    nt = rows // TM
    sub = next(s for s in FFN_TILES_PER_BLOCK if nt % s == 0)
    br = sub * TM
    hb = br // HALO
    n_hblk = rows // HALO
    n_ctx_local = max(n_ctx_tiles - t_off, 0)
    starts = sorted({max(0 - t_off, 0), n_ctx_local})
    ends = sorted({e for e in (n_ctx_local - 1, nt - 1) if e >= 0})
    full = lambda a: pl.BlockSpec(a.shape, lambda b, t: (0,) * a.ndim, pipeline_mode=pl.Buffered(1))
    weights = (wl["ffn_up"], wl["conv_w"], wl["conv_b"], wl["ffn_down"], wl["ln2_g"], wl["ln2_b"])
    n_rows = TM + 2 * HALO
    return _pallas(
        functools.partial(_ffn_kernel, sub=sub, n_ctx_tiles=n_ctx_local, seq_starts=tuple(starts),
                          seq_ends=tuple(ends)),
        name="conv_ffn_ln2",
        grid=(bsz, nt // sub),
        operands=(x1, x1, x1, mod_l, *weights),
        in_specs=[pl.BlockSpec((1, HALO, d), lambda b, t: (b, jnp.maximum(t * hb - 1, 0), 0)),
                  pl.BlockSpec((1, br, d), lambda b, t: (b, t, 0)),
                  pl.BlockSpec((1, HALO, d), lambda b, t: (b, jnp.minimum((t + 1) * hb, n_hblk - 1), 0)),
                  pl.BlockSpec((2, 1, 6 * d), lambda b, t: (b, 0, 0))]
                 + [full(w) for w in weights],
        out_specs=pl.BlockSpec((1, br, d), lambda b, t: (b, t, 0)),
        out_shape=jax.ShapeDtypeStruct((bsz, rows, d), F32),
        scratch_shapes=[pltpu.VMEM((br + 2 * HALO, d), BF16),
                        pltpu.VMEM((2, n_rows, FF_CHUNK), F32), pltpu.VMEM((2, n_rows, FF_CHUNK), F32),
                        pltpu.VMEM((2, TM, D_FF), BF16)],
    )


def _rope_tables(seq, n_ctx):
    pos = jnp.arange(seq, dtype=F32)
    ret_inv = 1.0 / (ROPE_BASE ** jnp.linspace(0.0, 1.0, DK // 2, dtype=F32))
    ret_ang = pos[:, None] * ret_inv
    rc, rs = jnp.cos(ret_ang), jnp.sin(ret_ang)
    ret_cos = jnp.tile(jnp.concatenate([rc, rc], axis=1), (1, N_HEADS))
    ret_sin = jnp.tile(jnp.concatenate([-rs, rs], axis=1), (1, N_HEADS))

    rows = jnp.repeat(jnp.arange(seq // GRID_W, dtype=F32), GRID_W)
    cols = jnp.tile(jnp.arange(GRID_W, dtype=F32), seq // GRID_W)
    n_ax = MLA_D_ROPE // 4
    ax_inv = ROPE_BASE ** (-jnp.arange(n_ax, dtype=F32) / n_ax)
    ra, ca = rows[:, None] * ax_inv, cols[:, None] * ax_inv
    cos32 = jnp.concatenate([jnp.cos(ra), jnp.cos(ra), jnp.cos(ca), jnp.cos(ca)], axis=1)
    sin32 = jnp.concatenate([-jnp.sin(ra), jnp.sin(ra), -jnp.sin(ca), jnp.sin(ca)], axis=1)
    one, zero = jnp.ones((seq, MLA_D_NOPE), F32), jnp.zeros((seq, 32), F32)
    q_cos = MLA_QSCALE * jnp.concatenate([one, cos32, zero], axis=1)
    q_sin = MLA_QSCALE * jnp.concatenate([0.0 * one, sin32, zero], axis=1)
    misc = jnp.concatenate([cos32, sin32, zero, zero], axis=1)
    lat = jnp.concatenate([ret_cos, ret_sin, q_cos, q_sin, misc], axis=1)

    c1, c0 = jnp.ones((n_ctx, 128), F32), jnp.zeros((n_ctx, 128), F32)
    q_cos_c = MLA_QSCALE * jnp.concatenate([jnp.ones((n_ctx, 96), F32), jnp.zeros((n_ctx, 32), F32)], axis=1)
    misc_c = jnp.concatenate([jnp.ones((n_ctx, 32), F32), jnp.zeros((n_ctx, 96), F32)], axis=1)
    ctx = jnp.concatenate([c1, c0, q_cos_c, c0, misc_c], axis=1)
    return jnp.concatenate([ctx, lat], axis=0)


def _swap_idx(n, half):
    j = np.arange(n)
    return (j // (2 * half)) * (2 * half) + (j % (2 * half) + half) % (2 * half)


def _layer_weights(i, w_in, gla_gate_w, gla_gate_b, gla_norm_g, ret_decay, mla_q_norm_g, mla_kv_norm_g,
                   mla_w_uq, mla_w_uk, mla_w_uv, w_out, ln1_g, ln1_b, ffn_up, ffn_conv_w, ffn_conv_b,
                   ffn_down, ln2_g, ln2_b):
    d = D_MODEL
    o = IN_OFFS
    wi = w_in[i]
    sec = lambda k: wi[:, o[k]:o[k + 1]]
    swap_ret = _swap_idx(QK_W, DK // 2)
    swap_kr = _swap_idx(MLA_D_ROPE, MLA_D_ROPE // 4)
    kr = sec(11)
    misc = jnp.concatenate([kr, kr[:, swap_kr], sec(3), jnp.zeros((d, 32), F32)], axis=1)
    w_in_p = jnp.concatenate([sec(0), sec(1), sec(2), sec(4), sec(5), sec(6), sec(7), sec(8), sec(9),
                              sec(10), misc, sec(5)[:, swap_ret], sec(6)[:, swap_ret]], axis=1).astype(BF16)

    wg = jnp.zeros((128, 2 * QK_W), F32)
    wg = wg.at[64:64 + GATE_RANK, 0:QK_W].set(gla_gate_w[i, 0])
    wg = wg.at[64 + GATE_RANK:64 + 2 * GATE_RANK, QK_W:].set(gla_gate_w[i, 1])
    bg = jnp.concatenate([gla_gate_b[i, 0], gla_gate_b[i, 1]])[None, :]

    uq = mla_w_uq[i].reshape(MLA_Q_RANK, MLA_HEADS, MLA_D_NOPE + MLA_D_ROPE)
    z32 = jnp.zeros((MLA_Q_RANK, MLA_HEADS, 32), F32)
    z64 = jnp.zeros((MLA_Q_RANK, MLA_HEADS, 64), F32)
    uq_main = jnp.concatenate([uq, z32], axis=2).reshape(MLA_Q_RANK, -1)
    uq_sw = jnp.concatenate([z64, uq[:, :, MLA_D_NOPE:][:, :, swap_kr], z32], axis=2).reshape(MLA_Q_RANK, -1)
    w_uq = jnp.concatenate([uq_main, uq_sw], axis=1).astype(BF16)

    uk = mla_w_uk[i].reshape(MLA_KV_RANK, MLA_HEADS, MLA_D_NOPE)
    uk_pad = jnp.concatenate([uk, jnp.zeros((MLA_KV_RANK, MLA_HEADS, 64), F32)], axis=2).reshape(MLA_KV_RANK, -1)
    uv = mla_w_uv[i].reshape(MLA_KV_RANK, MLA_HEADS, MLA_DV)
    even = (jnp.arange(MLA_HEADS) % 2 == 0)[None, :, None]
    uv_a = jnp.where(even, uv, 0.0).reshape(MLA_KV_RANK, -1)
    uv_b = jnp.where(even, 0.0, uv).reshape(MLA_KV_RANK, -1)
    place = jnp.zeros((128, MLA_HEADS, MLA_HEAD_PAD), F32)
    eye = jnp.eye(MLA_D_ROPE, dtype=F32)
    place = place.at[0:MLA_D_ROPE, :, MLA_D_NOPE:MLA_D_NOPE + MLA_D_ROPE].set(
        jnp.broadcast_to(eye[:, None, :], (MLA_D_ROPE, MLA_HEADS, MLA_D_ROPE)))
    place = place.reshape(128, -1)
    w_kv = jnp.concatenate([
        jnp.concatenate([uk_pad, uv_a, uv_b], axis=1),
        jnp.concatenate([place, jnp.zeros((128, 2 * MLA_HEADS * MLA_DV), F32)], axis=1)], axis=0).astype(BF16)

    row = lambda a: a[None, :]
    return {
        "w_in": w_in_p, "w_gate": wg.astype(BF16), "b_gate": bg,
        "qn_g": row(mla_q_norm_g[i]), "kvn_g": row(mla_kv_norm_g[i]), "w_uq": w_uq, "w_kv": w_kv,
        "gla_ng": row(jnp.tile(gla_norm_g[i], N_HEADS)),
        "dec_head": jnp.broadcast_to(ret_decay[i][:, :, None, None], (2, N_HEADS, 1, SCAN_CHUNK)),
        "dec_lanes": jnp.repeat(ret_decay[i], DK, axis=1)[:, None, :],
        "w_out": w_out[i].astype(BF16), "ln1_g": row(ln1_g[i]), "ln1_b": row(ln1_b[i]),
        "ffn_up": ffn_up[i].astype(BF16), "conv_w": ffn_conv_w[i], "conv_b": row(ffn_conv_b[i]),
        "ffn_down": ffn_down[i].astype(BF16), "ln2_g": row(ln2_g[i]), "ln2_b": row(ln2_b[i]),
    }


def kernel(x, c, ctx, c_ctx, ada_w, ada_b, w_in, gla_gate_w, gla_gate_b, gla_norm_g, ret_decay, mla_q_norm_g, mla_kv_norm_g, mla_w_uq, mla_w_uk, mla_w_uv, w_out, ln1_g, ln1_b, ffn_up, ffn_conv_w, ffn_conv_b, ffn_down, ln2_g, ln2_b):
    bsz, seq, d = x.shape
    n_ctx = ctx.shape[1]
    assert d == D_MODEL and seq % TM == 0 and n_ctx % TM == 0 and seq % GRID_W == 0
    n_ctx_tiles = n_ctx // TM

    rows = -(-(bsz + 1) // 8) * 8
    c_all = jnp.concatenate([c, c_ctx[None, :], jnp.zeros((rows - bsz - 1, d), F32)], axis=0)
    mod = _ada_call(c_all, ada_w, ada_b)
    mod_ctx = jnp.broadcast_to(mod[:, bsz:bsz + 1], (DEPTH, bsz, 6 * d))
    mod_sel = jnp.stack([mod_ctx, mod[:, :bsz]], axis=2).reshape(DEPTH, 2 * bsz, 1, 6 * d)

    tab = _rope_tables(seq, n_ctx)
    xa = jnp.concatenate([ctx, x], axis=1)
    dmat_scratch = [pltpu.VMEM((2, N_HEADS, SCAN_CHUNK, SCAN_CHUNK), F32)]

    for i in range(DEPTH):
        wl = _layer_weights(i, w_in, gla_gate_w, gla_gate_b, gla_norm_g, ret_decay, mla_q_norm_g,
                            mla_kv_norm_g, mla_w_uq, mla_w_uk, mla_w_uv, w_out, ln1_g, ln1_b, ffn_up,
                            ffn_conv_w, ffn_conv_b, ffn_down, ln2_g, ln2_b)
        mod_l = mod_sel[i]
        need_ctx = i < DEPTH - 1
        t_off = 0 if need_ctx else n_ctx_tiles
        g1, gv, gg, r1, rv, rg, mq, mkv = _proj_call(xa, mod_l, tab, wl, n_ctx_tiles)
        m_gla = _scan_call(_gla_kernel, "gla_scan", (g1, gv, gg), (wl["gla_ng"],), [], n_ctx)
        m_ret = _scan_call(_ret_kernel, "ret_scan", (r1, rv, rg), (wl["dec_head"], wl["dec_lanes"]),
                           dmat_scratch, n_ctx)
        x1 = _mla_call(mq, mkv, xa, m_gla, m_ret, mod_l, wl, n_ctx, t_off)
        xa = _ffn_call(x1, mod_l, wl, n_ctx_tiles, t_off)
    return xa
```

```python
import functools

import numpy as np
import jax
import jax.numpy as jnp
from jax import lax
from jax.experimental import pallas as pl
from jax.experimental.pallas import tpu as pltpu

F32 = jnp.float32
BF16 = jnp.bfloat16

D_MODEL = 1024
DEPTH = 4
GRID_W = 64
N_HEADS = 4
DK = 32
DV = 64
QK_W = N_HEADS * DK
V_W = N_HEADS * DV
GATE_RANK = 16
GLA_TAU = 16.0
MLA_HEADS = 8
MLA_D_NOPE = 64
MLA_D_ROPE = 32
MLA_DV = 64
MLA_Q_RANK = 256
MLA_KV_RANK = 128
MLA_HEAD_PAD = 128
MLA_SCALE = (MLA_D_NOPE + MLA_D_ROPE) ** -0.5
MLA_QSCALE = MLA_SCALE * float(np.log2(np.e))
D_FF = 2816
ROPE_BASE = 10000.0
EPS = 1e-6
ALPHA = (2 * DEPTH) ** 0.25
IN_SIZES = (QK_W, QK_W, V_W, 2 * GATE_RANK, V_W, QK_W, QK_W, V_W, V_W,
            MLA_Q_RANK, MLA_KV_RANK, MLA_D_ROPE)
IN_OFFS = tuple(int(o) for o in np.concatenate([[0], np.cumsum(IN_SIZES)]))

P_GQ, P_GK, P_GV, P_GG = 0, 128, 256, 512
P_RQ, P_RK, P_RV, P_RG = 768, 896, 1024, 1280
P_CQ, P_CKV, P_MISC = 1536, 1792, 1920
P_RQS, P_RKS = 2048, 2176
P_W = 2304

TM = 256
SCAN_CHUNK = 256
GLA_MAX_CHUNK_LOG_DECAY = 64.0
HALO = 8
FF_CHUNK = 256
FFN_TILES_PER_BLOCK = (3, 4, 2, 1)
DOWN_EVERY = 6
V7X_VMEM_BYTES = 64 * 1024 * 1024
VMEM_RESERVE_BYTES = 4 * 1024 * 1024
VMEM_VALUES_BYTES = 16 * 1024 * 1024


def _nbytes(shape, dtype):
    return int(np.prod(shape)) * jnp.dtype(dtype).itemsize


def _pallas(kernel, *, name, grid, in_specs, operands, out_specs, out_shape, scratch_shapes=()):
    out_list = out_shape if isinstance(out_shape, (list, tuple)) else [out_shape]
    spec_list = out_specs if isinstance(out_specs, (list, tuple)) else [out_specs]

    def block_bytes(spec, dtype):
        buffers = 2 if spec.pipeline_mode is None else spec.pipeline_mode.buffer_count
        return buffers * _nbytes(spec.block_shape, dtype)

    resident = (sum(block_bytes(s, a.dtype) for s, a in zip(in_specs, operands))
                + sum(block_bytes(s, o.dtype) for s, o in zip(spec_list, out_list))
                + sum(_nbytes(s.shape, s.dtype) for s in scratch_shapes))
    limit = min(resident + VMEM_VALUES_BYTES, V7X_VMEM_BYTES - VMEM_RESERVE_BYTES)
    return pl.pallas_call(
        kernel, grid=grid, in_specs=in_specs, out_specs=out_specs, out_shape=out_shape,
        scratch_shapes=list(scratch_shapes),
        compiler_params=pltpu.CompilerParams(
            dimension_semantics=("parallel",) + ("arbitrary",) * (len(grid) - 1), vmem_limit_bytes=limit),
        name=name,
    )(*operands)


def _iota(shape, dim):
    return lax.broadcasted_iota(jnp.int32, shape, dim)


def _split3(x):
    hi = x.astype(BF16)
    r1 = x - hi.astype(F32)
    mid = r1.astype(BF16)
    lo = (r1 - mid.astype(F32)).astype(BF16)
    return hi, mid, lo


def _dot_exact_rhs01(x, m01):
    hi, mid, lo = _split3(x)
    d = lambda t: jnp.dot(t, m01, preferred_element_type=F32)
    return d(hi) + d(mid) + d(lo)


def _log_sigmoid(z):
    return jnp.minimum(z, 0.0) - jnp.log1p(jnp.exp(-jnp.abs(z)))


def _silu(x):
    return x * jax.nn.sigmoid(x)


def _layer_norm(x, g, b):
    mu = jnp.mean(x, axis=-1, keepdims=True)
    xc = x - mu
    var = jnp.mean(xc * xc, axis=-1, keepdims=True)
    return xc * lax.rsqrt(var + EPS) * g + b


def _rms_norm(x, g):
    return x * lax.rsqrt(jnp.mean(x * x, axis=-1, keepdims=True) + EPS) * g


def _dot_nt(a, b):
    return lax.dot_general(a, b, (((1,), (1,)), ((), ())), preferred_element_type=F32)


def _dot_tn(a, b):
    return lax.dot_general(a, b, (((0,), (0,)), ((), ())), preferred_element_type=F32)


def _ada_kernel(c_ref, w_ref, b_ref, o_ref):
    a = _silu(c_ref[...]).astype(BF16)
    o_ref[0] = jnp.dot(a, w_ref[0].astype(BF16), preferred_element_type=F32) + b_ref[0]


def _ada_call(c_all, ada_w, ada_b):
    depth, d, e = ada_w.shape
    rows = c_all.shape[0]
    bn = 1024
    return _pallas(
        _ada_kernel,
        name="ada_mod",
        grid=(depth, e // bn),
        in_specs=[pl.BlockSpec((rows, d), lambda l, n: (0, 0)),
                  pl.BlockSpec((1, d, bn), lambda l, n: (l, 0, n)),
                  pl.BlockSpec((1, 1, bn), lambda l, n: (l, 0, n))],
        operands=(c_all, ada_w, ada_b.reshape(depth, 1, e)),
        out_specs=pl.BlockSpec((1, rows, bn), lambda l, n: (l, 0, n)),
        out_shape=jax.ShapeDtypeStruct((depth, rows, e), F32),
    )


def _proj_kernel(x_ref, mod_ref, tab_ref, w_in_ref, w_gate_ref, b_gate_ref, qn_g_ref, kvn_g_ref,
                 w_uq_ref, w_kv_ref,
                 g1_ref, gv_ref, gg_ref, r1_ref, rv_ref, rg_ref, mq_ref, mkv_ref):
    d = D_MODEL
    x = x_ref[0]
    mod = mod_ref[0]
    h = (x * (1.0 + mod[:, d:2 * d]) + mod[:, 0:d]).astype(BF16)
    p = jnp.dot(h, w_in_ref[...], preferred_element_type=F32)

    ret_cos = tab_ref[:, 0:128]
    ret_sin = tab_ref[:, 128:256]
    q_cos = tab_ref[:, 256:384]
    q_sin = tab_ref[:, 384:512]
    misc_tab = tab_ref[:, 512:640]

    misc = p[:, P_MISC:P_MISC + 128]
    z = jnp.dot(misc.astype(BF16), w_gate_ref[...], preferred_element_type=F32) + b_gate_ref[...]
    log_a = _log_sigmoid(z) / GLA_TAU
    g1_ref[0, :, 0:128] = p[:, P_GQ:P_GQ + 128] * (DK ** -0.5)
    g1_ref[0, :, 128:256] = p[:, P_GK:P_GK + 128]
    g1_ref[0, :, 256:512] = log_a
    gv_ref[0] = p[:, P_GV:P_GV + V_W].astype(BF16)
    gg_ref[0] = p[:, P_GG:P_GG + V_W]

    r1_ref[0, :, 0:128] = p[:, P_RQ:P_RQ + 128] * ret_cos + p[:, P_RQS:P_RQS + 128] * ret_sin
    ks = DK ** -0.5
    r1_ref[0, :, 128:256] = ((p[:, P_RK:P_RK + 128] * ks) * ret_cos
                             + (p[:, P_RKS:P_RKS + 128] * ks) * ret_sin)
    rv_ref[0] = p[:, P_RV:P_RV + V_W].astype(BF16)
    rg_ref[0] = p[:, P_RG:P_RG + V_W]

    cq = _rms_norm(p[:, P_CQ:P_CQ + MLA_Q_RANK], qn_g_ref[...]).astype(BF16)
    qq = jnp.dot(cq, w_uq_ref[...], preferred_element_type=F32)
    qw = MLA_HEADS * MLA_HEAD_PAD
    for hh in range(MLA_HEADS):
        lo = hh * MLA_HEAD_PAD
        mq_ref[0, :, lo:lo + MLA_HEAD_PAD] = (
            qq[:, lo:lo + MLA_HEAD_PAD] * q_cos + qq[:, qw + lo:qw + lo + MLA_HEAD_PAD] * q_sin
        ).astype(BF16)

    ckv = _rms_norm(p[:, P_CKV:P_CKV + MLA_KV_RANK], kvn_g_ref[...])
    prod = misc * misc_tab
    kr = prod + pltpu.roll(prod, 128 - MLA_D_ROPE, axis=1)
    lhs = jnp.concatenate([ckv, kr], axis=1).astype(BF16)
    mkv_ref[0] = jnp.dot(lhs, w_kv_ref[...], preferred_element_type=F32).astype(BF16)


def _proj_call(xa, mod_l, tab, wl, n_ctx_tiles):
    bsz, t_all, d = xa.shape
    nt = t_all // TM
    tile = lambda w: pl.BlockSpec((1, TM, w), lambda b, t: (b, t, 0))
    full = lambda a: pl.BlockSpec(a.shape, lambda b, t: (0,) * a.ndim)
    sds = lambda w, dt: jax.ShapeDtypeStruct((bsz, t_all, w), dt)
    weights = (wl["w_in"], wl["w_gate"], wl["b_gate"], wl["qn_g"], wl["kvn_g"], wl["w_uq"], wl["w_kv"])
    return _pallas(
        _proj_kernel,
        name="proj",
        grid=(bsz, nt),
        in_specs=[tile(d),
                  pl.BlockSpec((1, 1, 6 * d), lambda b, t: (2 * b + (t >= n_ctx_tiles).astype(jnp.int32), 0, 0)),
                  pl.BlockSpec((TM, tab.shape[1]), lambda b, t: (t, 0))]
                 + [full(w) for w in weights],
        operands=(xa, mod_l, tab, *weights),
        out_specs=[tile(512), tile(V_W), tile(V_W), tile(256), tile(V_W), tile(V_W),
                   tile(MLA_HEADS * MLA_HEAD_PAD), tile(2048)],
        out_shape=[sds(512, F32), sds(V_W, BF16), sds(V_W, F32), sds(256, F32), sds(V_W, BF16),
                   sds(V_W, F32), sds(MLA_HEADS * MLA_HEAD_PAD, BF16), sds(2048, BF16)],
    )


def _key_lane_masks():
    lane_head = _iota((1, QK_W), 1) // DK
    return [jnp.where(lane_head == h, 1.0, 0.0).astype(BF16) for h in range(N_HEADS)]


def _state_mask():
    return jnp.where((_iota((V_W, QK_W), 0) // DV) == (_iota((V_W, QK_W), 1) // DK), 1.0, 0.0)


def _fill_head_values(v_ref, v4_ref, n_rows):
    lane_head = _iota((TM, V_W), 1) // DV

    def body(i, carry):
        r0 = pl.multiple_of(i * TM, TM)
        v = v_ref[0, pl.ds(r0, TM), :].astype(F32)
        for h in range(N_HEADS):
            v4_ref[h, pl.ds(r0, TM), :] = jnp.where(lane_head == h, v, 0.0).astype(BF16)
        return carry
    lax.fori_loop(0, n_rows // TM, body, 0)


def _chunk_step(qe, ke, weights, qdec, kdec, sdec, s, v_ref, v4_ref, r0, key_masks, state_mask):
    c = SCAN_CHUNK
    o = None
    for h in range(N_HEADS):
        att = _dot_nt(qe * key_masks[h], ke) * weights[h]
        part = jnp.dot(att.astype(BF16), v4_ref[h, pl.ds(r0, c), :], preferred_element_type=F32)
        o = part if o is None else o + part
    o = o + _dot_nt(qdec, s.astype(BF16))
    upd = _dot_tn(v_ref[0, pl.ds(r0, c), :], kdec)
    return o, s * sdec + upd * state_mask


def _run_scan(chunk_fn, sf_ref, sb_ref, of_ref, ob_ref, n_ctx_chunks, n_chunks):
    s_f = jnp.zeros(sf_ref.shape, F32)
    s_b = jnp.zeros(sb_ref.shape, F32)
    for i in range(n_chunks):
        cb = n_ctx_chunks - 1 - i if i < n_ctx_chunks else n_chunks + n_ctx_chunks - 1 - i
        o_f, s_f = chunk_fn(i * SCAN_CHUNK, True, s_f)
        o_b, s_b = chunk_fn(cb * SCAN_CHUNK, False, s_b)
        of_ref[i * SCAN_CHUNK:(i + 1) * SCAN_CHUNK, :] = o_f
        ob_ref[cb * SCAN_CHUNK:(cb + 1) * SCAN_CHUNK, :] = o_b


def _group_ones():
    same = (_iota((V_W, V_W), 0) // DV) == (_iota((V_W, V_W), 1) // DV)
    return jnp.where(same, 1.0, 0.0).astype(BF16)


def _gla_kernel(g1_ref, gv_ref, gg_ref, ng_ref, out_ref, of_ref, ob_ref, sf_ref, sb_ref, v4_ref, *,
                n_ctx_chunks, n_chunks):
    c = SCAN_CHUNK
    key_masks = _key_lane_masks()
    state_mask = _state_mask()
    ti, si = _iota((c, c), 0), _iota((c, c), 1)
    causal = {True: jnp.where(si <= ti, 1.0, 0.0), False: jnp.where(si >= ti, 1.0, 0.0)}
    tri = {fwd: m.astype(BF16) for fwd, m in causal.items()}
    _fill_head_values(gv_ref, v4_ref, n_chunks * c)

    def chunk_fn(r0, fwd, s):
        q = g1_ref[0, pl.ds(r0, c), 0:128]
        k = g1_ref[0, pl.ds(r0, c), 128:256]
        a = g1_ref[0, pl.ds(r0, c), 256:384] if fwd else g1_ref[0, pl.ds(r0, c), 384:512]
        a_hi = a.astype(BF16)
        a_lo = (a - a_hi.astype(F32)).astype(BF16)
        b = (jnp.dot(tri[fwd], a_hi, preferred_element_type=F32)
             + jnp.dot(tri[fwd], a_lo, preferred_element_type=F32))
        b_mid = b[c // 2:c // 2 + 1, :]
        b_end = b[c - 1:c, :] if fwd else b[0:1, :]
        qe = (q * jnp.exp(b - b_mid)).astype(BF16)
        ke = (k * jnp.exp(b_mid - b)).astype(BF16)
        qdec = (q * jnp.exp(b)).astype(BF16)
        kdec = (k * jnp.exp(b_end - b)).astype(BF16)
        return _chunk_step(qe, ke, [causal[fwd]] * N_HEADS, qdec, kdec, jnp.exp(b_end), s,
                           gv_ref, v4_ref, r0, key_masks, state_mask)

    def token_scan():
        of_ref[...] = jnp.zeros_like(of_ref)
        ob_ref[...] = jnp.zeros_like(ob_ref)
        sf_ref[...] = jnp.zeros_like(sf_ref)
        sb_ref[...] = jnp.zeros_like(sb_ref)
        row = _iota((c, 1), 0)

        def token(r0, t, fwd):
            s_ref, o_ref = (sf_ref, of_ref) if fwd else (sb_ref, ob_ref)
            pick = row == t
            q = jnp.where(pick, g1_ref[0, pl.ds(r0, c), 0:128], 0.0).astype(BF16)
            k = jnp.where(pick, g1_ref[0, pl.ds(r0, c), 128:256], 0.0).astype(BF16)
            a = g1_ref[0, pl.ds(r0, c), 256:384] if fwd else g1_ref[0, pl.ds(r0, c), 384:512]
            a_t = jnp.sum(jnp.where(pick, a, 0.0), axis=0, keepdims=True)
            s = s_ref[...] * jnp.exp(a_t) + _dot_tn(gv_ref[0, pl.ds(r0, c), :], k) * state_mask
            s_ref[...] = s
            o_ref[pl.ds(r0, c), :] += _dot_nt(q, s.astype(BF16))

        def chunk_body(i, carry):
            rf = pl.multiple_of(i * c, c)
            cb = jnp.where(i < n_ctx_chunks, n_ctx_chunks - 1 - i, n_chunks + n_ctx_chunks - 1 - i)
            rb = pl.multiple_of(cb * c, c)

            def token_body(j, carry2):
                token(rf, j, True)
                token(rb, c - 1 - j, False)
                return carry2
            return lax.fori_loop(0, c, token_body, carry)
        lax.fori_loop(0, n_chunks, chunk_body, 0)

    def chunk_total(i, m):
        r0 = pl.multiple_of(i * c, c)
        return jnp.maximum(m, -jnp.sum(g1_ref[0, pl.ds(r0, c), 256:512], axis=0, keepdims=True))
    worst = jnp.max(lax.fori_loop(0, n_chunks, chunk_total, jnp.zeros((1, 2 * QK_W), F32)))
    chunked_ok = worst < GLA_MAX_CHUNK_LOG_DECAY

    @pl.when(chunked_ok)
    def _():
        _run_scan(chunk_fn, sf_ref, sb_ref, of_ref, ob_ref, n_ctx_chunks, n_chunks)

    @pl.when(jnp.logical_not(chunked_ok))
    def _():
        token_scan()

    ones_g = _group_ones()
    ng = ng_ref[...]

    for i in range((n_chunks * c) // TM):
        rows = slice(i * TM, (i + 1) * TM)
        o = of_ref[rows, :] + ob_ref[rows, :]
        ms = _dot_exact_rhs01(o * o, ones_g) * (1.0 / DV)
        y = o * lax.rsqrt(ms + EPS) * ng * _silu(gg_ref[0, rows, :])
        out_ref[0, rows, :] = y.astype(BF16)


def _ret_kernel(r1_ref, rv_ref, rg_ref, dec_head_ref, dec_lanes_ref, out_ref, of_ref, ob_ref, sf_ref,
                sb_ref, v4_ref, dm_ref, *, n_ctx_chunks, n_chunks):
    c = SCAN_CHUNK
    key_masks = _key_lane_masks()
    state_mask = _state_mask()
    ti, si = _iota((c, c), 0), _iota((c, c), 1)
    tl = _iota((c, QK_W), 0).astype(F32)
    _fill_head_values(rv_ref, v4_ref, n_chunks * c)

    consts = []
    for di in range(2):
        fwd = di == 0
        rel = (ti - si) if fwd else (si - ti)
        relf = jnp.maximum(rel, 0).astype(F32)
        for h in range(N_HEADS):
            lg = _log_sigmoid(dec_head_ref[di, h])
            dm_ref[di, h] = jnp.where(rel >= 0, jnp.exp(relf * lg), 0.0)
        lg_lanes = _log_sigmoid(dec_lanes_ref[di])
        n = tl if fwd else (c - 1.0) - tl
        consts.append((jnp.exp((n + 1.0) * lg_lanes), jnp.exp((c - 1.0 - n) * lg_lanes),
                       jnp.exp(float(c) * lg_lanes)))

    def chunk_fn(r0, fwd, s):
        di = 0 if fwd else 1
        q_dec, k_dec, s_dec = consts[di]
        q = r1_ref[0, pl.ds(r0, c), 0:128]
        k = r1_ref[0, pl.ds(r0, c), 128:256]
        return _chunk_step(q.astype(BF16), k.astype(BF16), [dm_ref[di, h] for h in range(N_HEADS)],
                           (q * q_dec).astype(BF16), (k * k_dec).astype(BF16), s_dec, s,
                           rv_ref, v4_ref, r0, key_masks, state_mask)

    _run_scan(chunk_fn, sf_ref, sb_ref, of_ref, ob_ref, n_ctx_chunks, n_chunks)

    ones_g = _group_ones()

    for i in range((n_chunks * c) // TM):
        rows = slice(i * TM, (i + 1) * TM)
        o = of_ref[rows, :] + ob_ref[rows, :]
        mu = _dot_exact_rhs01(o, ones_g) * (1.0 / DV)
        oc = o - mu
        var = _dot_exact_rhs01(oc * oc, ones_g) * (1.0 / DV)
        y = oc * lax.rsqrt(var + EPS) * _silu(rg_ref[0, rows, :])
        out_ref[0, rows, :] = y.astype(BF16)


def _scan_call(kernel, name, seq_inputs, small_inputs, extra_scratch, n_ctx):
    bsz, t_all, _ = seq_inputs[0].shape
    assert t_all % SCAN_CHUNK == 0 and n_ctx % SCAN_CHUNK == 0
    per_b = lambda a: pl.BlockSpec((1,) + a.shape[1:], lambda b: (b, 0, 0))
    full = lambda a: pl.BlockSpec(a.shape, lambda b: (0,) * a.ndim)
    state = pltpu.VMEM((V_W, QK_W), F32)
    o_dir = pltpu.VMEM((t_all, V_W), F32)
    return _pallas(
        functools.partial(kernel, n_ctx_chunks=n_ctx // SCAN_CHUNK, n_chunks=t_all // SCAN_CHUNK),
        name=name,
        grid=(bsz,),
        in_specs=[per_b(a) for a in seq_inputs] + [full(a) for a in small_inputs],
        operands=(*seq_inputs, *small_inputs),
        out_specs=pl.BlockSpec((1, t_all, V_W), lambda b: (b, 0, 0)),
        out_shape=jax.ShapeDtypeStruct((bsz, t_all, V_W), BF16),
        scratch_shapes=[o_dir, o_dir, state, state, pltpu.VMEM((N_HEADS, t_all, V_W), BF16)] + extra_scratch,
    )


def _mla_kernel(q_ref, kv_ref, x_ref, mg_ref, mr_ref, mod_ref, w_ref, g_ref, b_ref, o_ref, mm_ref, *,
                n_ctx, t_off):
    d = D_MODEL
    t = pl.program_id(1) + t_off
    kw = MLA_HEADS * MLA_HEAD_PAD
    vw = MLA_HEADS * MLA_DV
    first_half = _iota((1, 128), 1) < MLA_DV

    def attend(n_keys):
        for pair in range(MLA_HEADS // 2):
            pv, rinv = [], []
            for hh in range(2):
                lo = (2 * pair + hh) * MLA_HEAD_PAD
                s = _dot_nt(q_ref[0, :, lo:lo + MLA_HEAD_PAD], kv_ref[0, 0:n_keys, lo:lo + MLA_HEAD_PAD])
                e = jnp.exp2(s - jnp.max(s, axis=-1, keepdims=True))
                rinv.append(1.0 / jnp.sum(e, axis=-1, keepdims=True))
                vlo = kw + hh * vw + pair * 128
                pv.append(jnp.dot(e.astype(BF16), kv_ref[0, 0:n_keys, vlo:vlo + 128],
                                  preferred_element_type=F32))
            scale = jnp.where(first_half, rinv[0], rinv[1])
            mm_ref[:, pair * 128:(pair + 1) * 128] = ((pv[0] + pv[1]) * scale).astype(BF16)

    def project():
        w = V_W
        y = (jnp.dot(mg_ref[0], w_ref[0:w, :], preferred_element_type=F32)
             + jnp.dot(mr_ref[0], w_ref[w:2 * w, :], preferred_element_type=F32))
        y = y + jnp.dot(mm_ref[...], w_ref[2 * w:, :], preferred_element_type=F32)
        g1 = mod_ref[0][:, 2 * d:3 * d]
        o_ref[0] = _layer_norm(ALPHA * x_ref[0] + g1 * y, g_ref[...], b_ref[...])

    n_ctx_tiles = n_ctx // TM

    @pl.when(t < n_ctx_tiles)
    def _():
        attend(n_ctx)
        project()

    @pl.when(t >= n_ctx_tiles)
    def _():
        attend(kv_ref.shape[1])
        project()


def _mla_call(mq, mkv, xa, m_gla, m_ret, mod_l, wl, n_ctx, t_off):
    bsz, t_all, d = xa.shape
    nt = t_all // TM - t_off
    n_ctx_tiles = n_ctx // TM
    src = lambda w: pl.BlockSpec((1, TM, w), lambda b, t: (b, t + t_off, 0))
    full = lambda a: pl.BlockSpec(a.shape, lambda b, t: (0,) * a.ndim)
    return _pallas(
        functools.partial(_mla_kernel, n_ctx=n_ctx, t_off=t_off),
        name="mla_attn_out_ln1",
        grid=(bsz, nt),
        operands=(mq, mkv, xa, m_gla, m_ret, mod_l, wl["w_out"], wl["ln1_g"], wl["ln1_b"]),
        in_specs=[src(mq.shape[2]),
                  pl.BlockSpec((1, t_all, mkv.shape[2]), lambda b, t: (b, 0, 0)),
                  src(d), src(V_W), src(V_W),
                  pl.BlockSpec((1, 1, 6 * d),
                               lambda b, t: (2 * b + (t + t_off >= n_ctx_tiles).astype(jnp.int32), 0, 0)),
                  full(wl["w_out"]), full(wl["ln1_g"]), full(wl["ln1_b"])],
        out_specs=pl.BlockSpec((1, TM, d), lambda b, t: (b, t, 0)),
        out_shape=jax.ShapeDtypeStruct((bsz, nt * TM, d), F32),
        scratch_shapes=[pltpu.VMEM((TM, MLA_HEADS * MLA_DV), BF16)],
    )


def _ffn_kernel(xp_ref, x_ref, xn_ref, mod_ref, up_ref, cw_ref, cb_ref, down_ref, g_ref, b_ref, o_ref,
                h2_ref, ua_ref, ug_ref, act_ref, *, sub, n_ctx_tiles, seq_starts, seq_ends):
    d = D_MODEL
    n_rows = TM + 2 * HALO
    n_ff = D_FF // FF_CHUNK
    t = pl.program_id(1)
    mod_ctx, mod_lat = mod_ref[0], mod_ref[1]

    def tile_mod(tile):
        if n_ctx_tiles == 0:
            pick = lambda k: mod_lat[:, k * d:(k + 1) * d]
        else:
            is_ctx = tile < n_ctx_tiles
            pick = lambda k: jnp.where(is_ctx, mod_ctx[:, k * d:(k + 1) * d], mod_lat[:, k * d:(k + 1) * d])
        return pick(3), pick(4), pick(5)

    sh_l, sc_l = mod_lat[:, 3 * d:4 * d], mod_lat[:, 4 * d:5 * d]
    h2_ref[0:HALO, :] = (xp_ref[0] * (1.0 + sc_l) + sh_l).astype(BF16)
    h2_ref[HALO + sub * TM:2 * HALO + sub * TM, :] = (xn_ref[0] * (1.0 + sc_l) + sh_l).astype(BF16)
    gates = []
    for s in range(sub):
        sh2, sc2, g2 = tile_mod(t * sub + s)
        gates.append(g2)
        h2_ref[HALO + s * TM:HALO + (s + 1) * TM, :] = (
            x_ref[0, s * TM:(s + 1) * TM, :] * (1.0 + sc2) + sh2).astype(BF16)

    def edge_keep(s):
        tile = t * sub + s
        is_start = functools.reduce(jnp.logical_or, [tile == v for v in seq_starts])
        is_end = functools.reduce(jnp.logical_or, [tile == v for v in seq_ends])
        return jnp.where(is_start, 0.0, 1.0), jnp.where(is_end, 0.0, 1.0)
    keeps = [edge_keep(s) for s in range(sub)]

    def up_stage(s, j, slot):
        keep_prev, keep_next = keeps[s]
        for u_ref, lo in ((ua_ref, j * FF_CHUNK), (ug_ref, D_FF + j * FF_CHUNK)):
            u = jnp.dot(h2_ref[s * TM:s * TM + n_rows, :], up_ref[:, lo:lo + FF_CHUNK],
                        preferred_element_type=F32)
            u_ref[slot, 0:HALO, :] = u[0:HALO] * keep_prev
            u_ref[slot, HALO:HALO + TM, :] = u[HALO:HALO + TM]
            u_ref[slot, HALO + TM:n_rows, :] = u[HALO + TM:n_rows] * keep_next

    def conv(u_ref, slot, lo):
        cw = cw_ref[:, lo:lo + FF_CHUNK]
        u = u_ref[slot]
        prev = pltpu.roll(u * cw[0:1], 1, axis=0)[HALO:HALO + TM]
        nxt = pltpu.roll(u * cw[2:3], n_rows - 1, axis=0)[HALO:HALO + TM]
        return prev + (u[HALO:HALO + TM] * cw[1:2] + cb_ref[:, lo:lo + FF_CHUNK]) + nxt

    def act_stage(s, j, slot):
        a = conv(ua_ref, slot, j * FF_CHUNK)
        gt = conv(ug_ref, slot, D_FF + j * FF_CHUNK)
        act_ref[s % 2, :, j * FF_CHUNK:(j + 1) * FF_CHUNK] = (_silu(a) * gt).astype(BF16)

    items = [(s, j) for s in range(sub) for j in range(n_ff)]
    acc = None
    up_stage(0, 0, 0)
    for k, (s, j) in enumerate(items):
        if k + 1 < len(items):
            up_stage(*items[k + 1], (k + 1) % 2)
        act_stage(s, j, k % 2)
        down = lambda lo, hi: jnp.dot(act_ref[s % 2, :, lo:hi], down_ref[lo:hi, :], preferred_element_type=F32)
        if (j + 1) % DOWN_EVERY == 0 or j == n_ff - 1:
            lo = (j // DOWN_EVERY) * DOWN_EVERY * FF_CHUNK
            part = down(lo, (j + 1) * FF_CHUNK)
            acc = part if lo == 0 else acc + part
        if j == n_ff - 1:
            x1 = x_ref[0, s * TM:(s + 1) * TM, :]
            o_ref[0, s * TM:(s + 1) * TM, :] = _layer_norm(ALPHA * x1 + gates[s] * acc, g_ref[...], b_ref[...])


def _ffn_call(x1, mod_l, wl, n_ctx_tiles, t_off):
    bsz, rows, d = x1.shape
    nt = rows // TM
    sub = next(s for s in FFN_TILES_PER_BLOCK if nt % s == 0)
    br = sub * TM
    hb = br // HALO
    n_hblk = rows // HALO
    n_ctx_local = max(n_ctx_tiles - t_off, 0)
    starts = sorted({max(0 - t_off, 0), n_ctx_local})
    ends = sorted({e for e in (n_ctx_local - 1, nt - 1) if e >= 0})
    full = lambda a: pl.BlockSpec(a.shape, lambda b, t: (0,) * a.ndim, pipeline_mode=pl.Buffered(1))
    weights = (wl["ffn_up"], wl["conv_w"], wl["conv_b"], wl["ffn_down"], wl["ln2_g"], wl["ln2_b"])
    n_rows = TM + 2 * HALO
    return _pallas(
        functools.partial(_ffn_kernel, sub=sub, n_ctx_tiles=n_ctx_local, seq_starts=tuple(starts),
                          seq_ends=tuple(ends)),
        name="conv_ffn_ln2",
        grid=(bsz, nt // sub),
        operands=(x1, x1, x1, mod_l, *weights),
        in_specs=[pl.BlockSpec((1, HALO, d), lambda b, t: (b, jnp.maximum(t * hb - 1, 0), 0)),
                  pl.BlockSpec((1, br, d), lambda b, t: (b, t, 0)),
                  pl.BlockSpec((1, HALO, d), lambda b, t: (b, jnp.minimum((t + 1) * hb, n_hblk - 1), 0)),
                  pl.BlockSpec((2, 1, 6 * d), lambda b, t: (b, 0, 0))]
                 + [full(w) for w in weights],
        out_specs=pl.BlockSpec((1, br, d), lambda b, t: (b, t, 0)),
        out_shape=jax.ShapeDtypeStruct((bsz, rows, d), F32),
        scratch_shapes=[pltpu.VMEM((br + 2 * HALO, d), BF16),
                        pltpu.VMEM((2, n_rows, FF_CHUNK), F32), pltpu.VMEM((2, n_rows, FF_CHUNK), F32),
                        pltpu.VMEM((2, TM, D_FF), BF16)],
    )


def _rope_tables(seq, n_ctx):
    pos = jnp.arange(seq, dtype=F32)
    ret_inv = 1.0 / (ROPE_BASE ** jnp.linspace(0.0, 1.0, DK // 2, dtype=F32))
    ret_ang = pos[:, None] * ret_inv
    rc, rs = jnp.cos(ret_ang), jnp.sin(ret_ang)
    ret_cos = jnp.tile(jnp.concatenate([rc, rc], axis=1), (1, N_HEADS))
    ret_sin = jnp.tile(jnp.concatenate([-rs, rs], axis=1), (1, N_HEADS))

    rows = jnp.repeat(jnp.arange(seq // GRID_W, dtype=F32), GRID_W)
    cols = jnp.tile(jnp.arange(GRID_W, dtype=F32), seq // GRID_W)
    n_ax = MLA_D_ROPE // 4
    ax_inv = ROPE_BASE ** (-jnp.arange(n_ax, dtype=F32) / n_ax)
    ra, ca = rows[:, None] * ax_inv, cols[:, None] * ax_inv
    cos32 = jnp.concatenate([jnp.cos(ra), jnp.cos(ra), jnp.cos(ca), jnp.cos(ca)], axis=1)
    sin32 = jnp.concatenate([-jnp.sin(ra), jnp.sin(ra), -jnp.sin(ca), jnp.sin(ca)], axis=1)
    one, zero = jnp.ones((seq, MLA_D_NOPE), F32), jnp.zeros((seq, 32), F32)
    q_cos = MLA_QSCALE * jnp.concatenate([one, cos32, zero], axis=1)
    q_sin = MLA_QSCALE * jnp.concatenate([0.0 * one, sin32, zero], axis=1)
    misc = jnp.concatenate([cos32, sin32, zero, zero], axis=1)
    lat = jnp.concatenate([ret_cos, ret_sin, q_cos, q_sin, misc], axis=1)

    c1, c0 = jnp.ones((n_ctx, 128), F32), jnp.zeros((n_ctx, 128), F32)
    q_cos_c = MLA_QSCALE * jnp.concatenate([jnp.ones((n_ctx, 96), F32), jnp.zeros((n_ctx, 32), F32)], axis=1)
    misc_c = jnp.concatenate([jnp.ones((n_ctx, 32), F32), jnp.zeros((n_ctx, 96), F32)], axis=1)
    ctx = jnp.concatenate([c1, c0, q_cos_c, c0, misc_c], axis=1)
    return jnp.concatenate([ctx, lat], axis=0)


def _swap_idx(n, half):
    j = np.arange(n)
    return (j // (2 * half)) * (2 * half) + (j % (2 * half) + half) % (2 * half)


def _layer_weights(i, w_in, gla_gate_w, gla_gate_b, gla_norm_g, ret_decay, mla_q_norm_g, mla_kv_norm_g,
                   mla_w_uq, mla_w_uk, mla_w_uv, w_out, ln1_g, ln1_b, ffn_up, ffn_conv_w, ffn_conv_b,
                   ffn_down, ln2_g, ln2_b):
    d = D_MODEL
    o = IN_OFFS
    wi = w_in[i]
    sec = lambda k: wi[:, o[k]:o[k + 1]]
    swap_ret = _swap_idx(QK_W, DK // 2)
    swap_kr = _swap_idx(MLA_D_ROPE, MLA_D_ROPE // 4)
    kr = sec(11)
    misc = jnp.concatenate([kr, kr[:, swap_kr], sec(3), jnp.zeros((d, 32), F32)], axis=1)
    w_in_p = jnp.concatenate([sec(0), sec(1), sec(2), sec(4), sec(5), sec(6), sec(7), sec(8), sec(9),
                              sec(10), misc, sec(5)[:, swap_ret], sec(6)[:, swap_ret]], axis=1).astype(BF16)

    wg = jnp.zeros((128, 2 * QK_W), F32)
    wg = wg.at[64:64 + GATE_RANK, 0:QK_W].set(gla_gate_w[i, 0])
    wg = wg.at[64 + GATE_RANK:64 + 2 * GATE_RANK, QK_W:].set(gla_gate_w[i, 1])
    bg = jnp.concatenate([gla_gate_b[i, 0], gla_gate_b[i, 1]])[None, :]

    uq = mla_w_uq[i].reshape(MLA_Q_RANK, MLA_HEADS, MLA_D_NOPE + MLA_D_ROPE)
    z32 = jnp.zeros((MLA_Q_RANK, MLA_HEADS, 32), F32)
    z64 = jnp.zeros((MLA_Q_RANK, MLA_HEADS, 64), F32)
    uq_main = jnp.concatenate([uq, z32], axis=2).reshape(MLA_Q_RANK, -1)
    uq_sw = jnp.concatenate([z64, uq[:, :, MLA_D_NOPE:][:, :, swap_kr], z32], axis=2).reshape(MLA_Q_RANK, -1)
    w_uq = jnp.concatenate([uq_main, uq_sw], axis=1).astype(BF16)

    uk = mla_w_uk[i].reshape(MLA_KV_RANK, MLA_HEADS, MLA_D_NOPE)
    uk_pad = jnp.concatenate([uk, jnp.zeros((MLA_KV_RANK, MLA_HEADS, 64), F32)], axis=2).reshape(MLA_KV_RANK, -1)
    uv = mla_w_uv[i].reshape(MLA_KV_RANK, MLA_HEADS, MLA_DV)
    even = (jnp.arange(MLA_HEADS) % 2 == 0)[None, :, None]
    uv_a = jnp.where(even, uv, 0.0).reshape(MLA_KV_RANK, -1)
    uv_b = jnp.where(even, 0.0, uv).reshape(MLA_KV_RANK, -1)
    place = jnp.zeros((128, MLA_HEADS, MLA_HEAD_PAD), F32)
    eye = jnp.eye(MLA_D_ROPE, dtype=F32)
    place = place.at[0:MLA_D_ROPE, :, MLA_D_NOPE:MLA_D_NOPE + MLA_D_ROPE].set(
        jnp.broadcast_to(eye[:, None, :], (MLA_D_ROPE, MLA_HEADS, MLA_D_ROPE)))
    place = place.reshape(128, -1)
    w_kv = jnp.concatenate([
        jnp.concatenate([uk_pad, uv_a, uv_b], axis=1),
        jnp.concatenate([place, jnp.zeros((128, 2 * MLA_HEADS * MLA_DV), F32)], axis=1)], axis=0).astype(BF16)

    row = lambda a: a[None, :]
    return {
        "w_in": w_in_p, "w_gate": wg.astype(BF16), "b_gate": bg,
        "qn_g": row(mla_q_norm_g[i]), "kvn_g": row(mla_kv_norm_g[i]), "w_uq": w_uq, "w_kv": w_kv,
        "gla_ng": row(jnp.tile(gla_norm_g[i], N_HEADS)),
        "dec_head": jnp.broadcast_to(ret_decay[i][:, :, None, None], (2, N_HEADS, 1, SCAN_CHUNK)),
        "dec_lanes": jnp.repeat(ret_decay[i], DK, axis=1)[:, None, :],
        "w_out": w_out[i].astype(BF16), "ln1_g": row(ln1_g[i]), "ln1_b": row(ln1_b[i]),
        "ffn_up": ffn_up[i].astype(BF16), "conv_w": ffn_conv_w[i], "conv_b": row(ffn_conv_b[i]),
        "ffn_down": ffn_down[i].astype(BF16), "ln2_g": row(ln2_g[i]), "ln2_b": row(ln2_b[i]),
    }


def kernel(x, c, ctx, c_ctx, ada_w, ada_b, w_in, gla_gate_w, gla_gate_b, gla_norm_g, ret_decay, mla_q_norm_g, mla_kv_norm_g, mla_w_uq, mla_w_uk, mla_w_uv, w_out, ln1_g, ln1_b, ffn_up, ffn_conv_w, ffn_conv_b, ffn_down, ln2_g, ln2_b):
    bsz, seq, d = x.shape
    n_ctx = ctx.shape[1]
    assert d == D_MODEL and seq % TM == 0 and n_ctx % TM == 0 and seq % GRID_W == 0
    n_ctx_tiles = n_ctx // TM

    rows = -(-(bsz + 1) // 8) * 8
    c_all = jnp.concatenate([c, c_ctx[None, :], jnp.zeros((rows - bsz - 1, d), F32)], axis=0)
    mod = _ada_call(c_all, ada_w, ada_b)
    mod_ctx = jnp.broadcast_to(mod[:, bsz:bsz + 1], (DEPTH, bsz, 6 * d))
    mod_sel = jnp.stack([mod_ctx, mod[:, :bsz]], axis=2).reshape(DEPTH, 2 * bsz, 1, 6 * d)

    tab = _rope_tables(seq, n_ctx)
    xa = jnp.concatenate([ctx, x], axis=1)
    dmat_scratch = [pltpu.VMEM((2, N_HEADS, SCAN_CHUNK, SCAN_CHUNK), F32)]

    for i in range(DEPTH):
        wl = _layer_weights(i, w_in, gla_gate_w, gla_gate_b, gla_norm_g, ret_decay, mla_q_norm_g,
                            mla_kv_norm_g, mla_w_uq, mla_w_uk, mla_w_uv, w_out, ln1_g, ln1_b, ffn_up,
                            ffn_conv_w, ffn_conv_b, ffn_down, ln2_g, ln2_b)
        mod_l = mod_sel[i]
        need_ctx = i < DEPTH - 1
        t_off = 0 if need_ctx else n_ctx_tiles
        g1, gv, gg, r1, rv, rg, mq, mkv = _proj_call(xa, mod_l, tab, wl, n_ctx_tiles)
        m_gla = _scan_call(_gla_kernel, "gla_scan", (g1, gv, gg), (wl["gla_ng"],), [], n_ctx)
        m_ret = _scan_call(_ret_kernel, "ret_scan", (r1, rv, rg), (wl["dec_head"], wl["dec_lanes"]),
                           dmat_scratch, n_ctx)
        x1 = _mla_call(mq, mkv, xa, m_gla, m_ret, mod_l, wl, n_ctx, t_off)
        xa = _ffn_call(x1, mod_l, wl, n_ctx_tiles, t_off)
    return xa
```
